```python
import jax, jax.numpy as jnp
from jax import lax
import numpy as np

D_MODEL = 2048
BATCH = 4
SEQ = 2048
DEPTH = 4

CHUNK = 64
N_MIXERS = 4
D_FF = 4 * D_MODEL
NORM_EPS = 1e-6

RET_HEADS = 8
RET_DK = D_MODEL // RET_HEADS
RET_DV = 2 * D_MODEL // RET_HEADS
RET_IN = 2 * RET_HEADS * RET_DK + 2 * RET_HEADS * RET_DV
ROPE_BASE = 10000.0

GDN_HEADS = 16
GDN_DK = D_MODEL // GDN_HEADS
GDN_DV = D_MODEL // GDN_HEADS
GDN_QKV = GDN_HEADS * (2 * GDN_DK + GDN_DV)
GDN_IN = GDN_QKV + GDN_HEADS * GDN_DV + 2 * GDN_HEADS
CONV_WIDTH = 4

GLA_HEADS = 4
GLA_DK = D_MODEL // 2 // GLA_HEADS
GLA_DV = D_MODEL // GLA_HEADS
GLA_GATE_RANK = 16
GLA_TAU = 16.0
GLA_IN = 2 * GLA_HEADS * GLA_DK + 2 * GLA_HEADS * GLA_DV + GLA_GATE_RANK

LRU_WIDTH = D_MODEL
LRU_BLOCKS = 16
LRU_BLOCK = LRU_WIDTH // LRU_BLOCKS
LRU_C = 8.0

kernel_name = "interleaved_hybrid_chunk_causal_encoder"

F32 = jnp.float32


def _layers_of(m):
    return (DEPTH - m + N_MIXERS - 1) // N_MIXERS


def rmsnorm(x, g):
    xf = x.astype(F32)
    y = xf * lax.rsqrt(jnp.mean(xf * xf, axis=-1, keepdims=True) + NORM_EPS)
    return (y * g.astype(F32)).astype(x.dtype)


def head_norm(o, gain, center):
    of = o.astype(F32)
    if center:
        of = of - jnp.mean(of, axis=-1, keepdims=True)
    of = of * lax.rsqrt(jnp.mean(of * of, axis=-1, keepdims=True) + NORM_EPS)
    of = of * gain.astype(F32)
    return of.reshape(o.shape[0], o.shape[1], -1)


def l2norm(x):
    return x * lax.rsqrt(jnp.sum(x * x, axis=-1, keepdims=True) + NORM_EPS)


def causal_depthwise_conv(x, w):
    width, s = w.shape[0], x.shape[1]
    xp = jnp.pad(x, ((0, 0), (width - 1, 0), (0, 0)))
    out = xp[:, 0:s] * w[0]
    for tap in range(1, width):
        out = out + xp[:, tap:tap + s] * w[tap]
    return out


def to_chunks(x):
    b, s, h, d = x.shape
    return x.reshape(b, s // CHUNK, CHUNK, h, d).transpose(0, 3, 1, 2, 4)


def from_chunks(x):
    b, h, n, c, d = x.shape
    return x.transpose(0, 2, 3, 1, 4).reshape(b, n * c, h, d)


def scalar_chunks(x):
    b, s, h = x.shape
    return x.reshape(b, s // CHUNK, CHUNK, h).transpose(0, 3, 1, 2)


def rotary(x):
    s, d = x.shape[1], x.shape[-1]
    inv = ROPE_BASE ** (-jnp.arange(0, d, 2, dtype=F32) / d)
    ang = jnp.arange(s, dtype=F32)[:, None] * inv[None, :]
    cos, sin = jnp.cos(ang)[:, None, :], jnp.sin(ang)[:, None, :]
    x1, x2 = x[..., : d // 2], x[..., d // 2:]
    return jnp.concatenate([x1 * cos - x2 * sin, x1 * sin + x2 * cos], axis=-1)


def retention_mixer(h, w_in, gn_gain, w_out):
    b, s, _ = h.shape
    H, DK, DV = RET_HEADS, RET_DK, RET_DV
    q, k, v, g = jnp.split(h @ w_in, [H * DK, 2 * H * DK, 2 * H * DK + H * DV], axis=-1)
    q = rotary(q.astype(F32).reshape(b, s, H, DK))
    k = rotary(k.astype(F32).reshape(b, s, H, DK)) * DK ** -0.5
    v = v.astype(F32).reshape(b, s, H, DV)
    qc, kc, vc = to_chunks(q), to_chunks(k), to_chunks(v)
    log_gamma = jnp.log1p(-jnp.exp2(-5.0 - jnp.arange(H, dtype=F32)))
    pos = jnp.arange(CHUNK, dtype=F32)
    dist = jnp.abs(pos[:, None] - pos[None, :])
    decay_intra = jnp.exp(log_gamma[:, None, None] * dist)
    scores = jnp.einsum('bhncd,bhnmd->bhncm', qc, kc) * decay_intra[None, :, None]
    o_intra = jnp.einsum('bhncm,bhnmv->bhncv', scores, vc)
    q_decay = jnp.exp(log_gamma[:, None] * (pos + 1.0))[None, :, :, None]
    k_decay = jnp.exp(log_gamma[:, None] * (CHUNK - 1.0 - pos))[None, :, :, None]
    chunk_decay = jnp.exp(log_gamma * CHUNK)[None, :, None, None]

    def step(state, xs):
        q_n, k_n, v_n = xs
        o = jnp.einsum('bhcd,bhdv->bhcv', q_n * q_decay, state)
        state = state * chunk_decay + jnp.einsum('bhcd,bhcv->bhdv', k_n * k_decay, v_n)
        return state, o

    xs = (jnp.moveaxis(qc, 2, 0), jnp.moveaxis(kc, 2, 0), jnp.moveaxis(vc, 2, 0))
    _, o_inter = lax.scan(step, jnp.zeros((b, H, DK, DV), F32), xs)
    o = from_chunks(o_intra + jnp.moveaxis(o_inter, 0, 2))
    o = head_norm(o, gn_gain, center=True) * jax.nn.silu(g.astype(F32))
    return o.astype(h.dtype) @ w_out


def gated_deltanet_mixer(h, w_in, conv_w, a_log, dt_bias, norm_gain, w_out):
    b, s, _ = h.shape
    H, DK, DV = GDN_HEADS, GDN_DK, GDN_DV
    qkv, z, beta_logit, a_logit = jnp.split(
        h @ w_in, [GDN_QKV, GDN_QKV + H * DV, GDN_QKV + H * DV + H], axis=-1)
    qkv = jax.nn.silu(causal_depthwise_conv(qkv, conv_w)).astype(F32)
    q, k, v = jnp.split(qkv, [H * DK, 2 * H * DK], axis=-1)
    q = l2norm(q.reshape(b, s, H, DK)) * DK ** -0.5
    k = l2norm(k.reshape(b, s, H, DK))
    v = v.reshape(b, s, H, DV)
    beta = jax.nn.sigmoid(beta_logit.astype(F32))
    log_alpha = -jnp.exp(a_log.astype(F32)) * jax.nn.softplus(a_logit.astype(F32) + dt_bias.astype(F32))
    qc, kc, vc = to_chunks(q), to_chunks(k), to_chunks(v)
    beta_c = scalar_chunks(beta)
    cum = jnp.cumsum(scalar_chunks(log_alpha), axis=-1)
    idx = jnp.arange(CHUNK)
    strict = idx[:, None] > idx[None, :]
    rel = jnp.where(strict, jnp.exp(jnp.where(strict, cum[..., :, None] - cum[..., None, :], 0.0)), 0.0)
    a_mat = beta_c[..., :, None] * rel * jnp.einsum('bhncd,bhnmd->bhncm', kc, kc)
    l_mat = a_mat + jnp.eye(CHUNK, dtype=F32)
    rhs = jnp.concatenate([beta_c[..., None] * vc, (beta_c * jnp.exp(cum))[..., None] * kc], axis=-1)
    sol = lax.linalg.triangular_solve(l_mat, rhs, left_side=True, lower=True, unit_diagonal=True)
    u, w = sol[..., :DV], sol[..., DV:]
    k_end = kc * jnp.exp(cum[..., -1:] - cum)[..., None]
    trans = (jnp.exp(cum[..., -1])[..., None, None] * jnp.eye(DK, dtype=F32)
             - jnp.einsum('bhnck,bhncj->bhnkj', k_end, w))
    inject = jnp.einsum('bhnck,bhncv->bhnkv', k_end, u)

    def step(state, xs):
        t, g, q_n = xs
        state = jnp.einsum('bhkj,bhjv->bhkv', t, state) + g
        return state, jnp.einsum('bhck,bhkv->bhcv', q_n, state)

    xs = (jnp.moveaxis(trans, 2, 0), jnp.moveaxis(inject, 2, 0), jnp.moveaxis(qc, 2, 0))
    _, o = lax.scan(step, jnp.zeros((b, H, DK, DV), F32), xs)
    o = from_chunks(jnp.moveaxis(o, 0, 2))
    o = head_norm(o, norm_gain, center=False) * jax.nn.silu(z.astype(F32))
    return o.astype(h.dtype) @ w_out


def gla_mixer(h, w_in, w_gate_up, gate_bias, norm_gain, w_out):
    b, s, _ = h.shape
    H, DK, DV = GLA_HEADS, GLA_DK, GLA_DV
    QK, V = H * DK, H * DV
    q, k, v, r, gate_low = jnp.split(h @ w_in, [QK, 2 * QK, 2 * QK + V, 2 * QK + 2 * V], axis=-1)
    gate_logit = (gate_low @ w_gate_up + gate_bias).astype(F32)
    log_alpha = jax.nn.log_sigmoid(gate_logit) / GLA_TAU
    qc = to_chunks(q.astype(F32).reshape(b, s, H, DK)) * DK ** -0.5
    kc = to_chunks(k.astype(F32).reshape(b, s, H, DK))
    vc = to_chunks(v.astype(F32).reshape(b, s, H, DV))
    cum = jnp.cumsum(to_chunks(log_alpha.reshape(b, s, H, DK)), axis=-2)
    ref = cum[..., CHUNK // 2 - 1:CHUNK // 2, :]
    fwd, bwd = jnp.exp(cum - ref), jnp.exp(ref - cum)
    s_lo = jnp.einsum('bhnck,bhnmk->bhncm', qc * fwd, kc * bwd)
    s_up = jnp.einsum('bhnck,bhnmk->bhncm', qc * bwd, kc * fwd)
    idx = jnp.arange(CHUNK)
    scores = jnp.where(idx[:, None] >= idx[None, :], s_lo, s_up)
    o_intra = jnp.einsum('bhncm,bhnmv->bhncv', scores, vc)
    q_in = qc * jnp.exp(cum)
    k_end = kc * jnp.exp(cum[..., -1:, :] - cum)
    chunk_dec = jnp.exp(cum[..., -1, :])

    def step(state, xs):
        q_n, k_n, v_n, dec = xs
        o = jnp.einsum('bhck,bhkv->bhcv', q_n, state)
        state = state * dec[..., None] + jnp.einsum('bhck,bhcv->bhkv', k_n, v_n)
        return state, o

    xs = tuple(jnp.moveaxis(t, 2, 0) for t in (q_in, k_end, vc, chunk_dec))
    _, o_inter = lax.scan(step, jnp.zeros((b, H, DK, DV), F32), xs)
    o = from_chunks(o_intra + jnp.moveaxis(o_inter, 0, 2))
    o = head_norm(o, norm_gain, center=False) * jax.nn.silu(r.astype(F32))
    return o.astype(h.dtype) @ w_out


def rglru_mixer(h, w_in, conv_w, conv_b, w_rgate, b_rgate, w_igate, b_igate, lam, w_out):
    b, s, _ = h.shape
    xb, yb = jnp.split(h @ w_in, [LRU_WIDTH], axis=-1)
    yb = jax.nn.gelu(yb.astype(F32))
    xb = (causal_depthwise_conv(xb, conv_w) + conv_b).astype(F32)
    xblk = xb.reshape(b, s, LRU_BLOCKS, LRU_BLOCK)
    r = jax.nn.sigmoid(jnp.einsum('bsnd,nde->bsne', xblk, w_rgate.astype(F32)) + b_rgate.astype(F32))
    i = jax.nn.sigmoid(jnp.einsum('bsnd,nde->bsne', xblk, w_igate.astype(F32)) + b_igate.astype(F32))
    r, i = r.reshape(b, s, LRU_WIDTH), i.reshape(b, s, LRU_WIDTH)
    log_a = -LRU_C * jax.nn.softplus(-lam.astype(F32)) * r
    a = jnp.exp(log_a)
    u = jnp.sqrt(-jnp.expm1(2.0 * log_a)) * (i * xb)

    def combine(left, right):
        a_l, b_l = left
        a_r, b_r = right
        return a_l * a_r, a_r * b_l + b_r

    _, hs = lax.associative_scan(combine, (a, u), axis=1)
    return (hs * yb).astype(h.dtype) @ w_out


def squared_relu_mlp(h, w_up, w_down):
    return jnp.square(jax.nn.relu(h @ w_up)) @ w_down


def setup_inputs(seed: int = 0) -> dict:
    key = jax.random.key(seed)
    ks = iter(jax.random.split(key, 40))

    def nrm(shape, scale):
        return scale * jax.random.normal(next(ks), shape, F32)

    def uni(shape, lo, hi):
        return jax.random.uniform(next(ks), shape, F32, lo, hi)

    n_ret, n_gdn, n_gla, n_lru = (_layers_of(m) for m in range(N_MIXERS))
    d_in = D_MODEL ** -0.5
    dt = jnp.exp(uni((n_gdn, GDN_HEADS), float(np.log(1e-3)), float(np.log(1e-1))))
    a0 = uni((n_lru, LRU_WIDTH), 0.9, 0.999)
    s0 = a0 ** (1.0 / LRU_C)
    return {
        "x": nrm((BATCH, SEQ, D_MODEL), 1.0),
        "norm1": 1.0 + nrm((DEPTH, D_MODEL), 0.02),
        "norm2": 1.0 + nrm((DEPTH, D_MODEL), 0.02),
        "final_norm": 1.0 + nrm((D_MODEL,), 0.02),
        "ret_w_in": nrm((n_ret, D_MODEL, RET_IN), d_in),
        "ret_gn_gain": 1.0 + nrm((n_ret, RET_HEADS, RET_DV), 0.02),
        "ret_w_out": nrm((n_ret, RET_HEADS * RET_DV, D_MODEL), (RET_HEADS * RET_DV) ** -0.5),
        "gdn_w_in": nrm((n_gdn, D_MODEL, GDN_IN), d_in),
        "gdn_conv_w": nrm((n_gdn, CONV_WIDTH, GDN_QKV), CONV_WIDTH ** -0.5),
        "gdn_a_log": jnp.log(uni((n_gdn, GDN_HEADS), 1.0, 16.0)),
        "gdn_dt_bias": dt + jnp.log(-jnp.expm1(-dt)),
        "gdn_norm_gain": 1.0 + nrm((n_gdn, GDN_DV), 0.02),
        "gdn_w_out": nrm((n_gdn, GDN_HEADS * GDN_DV, D_MODEL), (GDN_HEADS * GDN_DV) ** -0.5),
        "gla_w_in": nrm((n_gla, D_MODEL, GLA_IN), d_in),
        "gla_w_gate_up": nrm((n_gla, GLA_GATE_RANK, GLA_HEADS * GLA_DK), GLA_GATE_RANK ** -0.5),
        "gla_gate_bias": nrm((n_gla, GLA_HEADS * GLA_DK), 0.1),
        "gla_norm_gain": 1.0 + nrm((n_gla, GLA_HEADS, GLA_DV), 0.02),
        "gla_w_out": nrm((n_gla, GLA_HEADS * GLA_DV, D_MODEL), (GLA_HEADS * GLA_DV) ** -0.5),
        "lru_w_in": nrm((n_lru, D_MODEL, 2 * LRU_WIDTH), d_in),
        "lru_conv_w": nrm((n_lru, CONV_WIDTH, LRU_WIDTH), CONV_WIDTH ** -0.5),
        "lru_conv_b": nrm((n_lru, LRU_WIDTH), 0.02),
        "lru_w_rgate": nrm((n_lru, LRU_BLOCKS, LRU_BLOCK, LRU_BLOCK), LRU_BLOCK ** -0.5),
        "lru_b_rgate": nrm((n_lru, LRU_BLOCKS, LRU_BLOCK), 0.02),
        "lru_w_igate": nrm((n_lru, LRU_BLOCKS, LRU_BLOCK, LRU_BLOCK), LRU_BLOCK ** -0.5),
        "lru_b_igate": nrm((n_lru, LRU_BLOCKS, LRU_BLOCK), 0.02),
        "lru_lambda": jnp.log(s0) - jnp.log1p(-s0),
        "lru_w_out": nrm((n_lru, LRU_WIDTH, D_MODEL), LRU_WIDTH ** -0.5),
        "mlp_w_up": nrm((DEPTH, D_MODEL, D_FF), d_in),
        "mlp_w_down": nrm((DEPTH, D_FF, D_MODEL), D_FF ** -0.5),
    }


def reference(x, norm1, norm2, final_norm,
              ret_w_in, ret_gn_gain, ret_w_out,
              gdn_w_in, gdn_conv_w, gdn_a_log, gdn_dt_bias, gdn_norm_gain, gdn_w_out,
              gla_w_in, gla_w_gate_up, gla_gate_bias, gla_norm_gain, gla_w_out,
              lru_w_in, lru_conv_w, lru_conv_b, lru_w_rgate, lru_b_rgate, lru_w_igate,
              lru_b_igate, lru_lambda, lru_w_out,
              mlp_w_up, mlp_w_down):
    for layer in range(DEPTH):
        m, j = layer % N_MIXERS, layer // N_MIXERS
        hn = rmsnorm(x, norm1[layer])
        if m == 0:
            y = retention_mixer(hn, ret_w_in[j], ret_gn_gain[j], ret_w_out[j])
        elif m == 1:
            y = gated_deltanet_mixer(hn, gdn_w_in[j], gdn_conv_w[j], gdn_a_log[j], gdn_dt_bias[j],
                                     gdn_norm_gain[j], gdn_w_out[j])
        elif m == 2:
            y = gla_mixer(hn, gla_w_in[j], gla_w_gate_up[j], gla_gate_bias[j], gla_norm_gain[j], gla_w_out[j])
        else:
            y = rglru_mixer(hn, lru_w_in[j], lru_conv_w[j], lru_conv_b[j], lru_w_rgate[j], lru_b_rgate[j],
                            lru_w_igate[j], lru_b_igate[j], lru_lambda[j], lru_w_out[j])
        x = x + y
        x = x + squared_relu_mlp(rmsnorm(x, norm2[layer]), mlp_w_up[layer], mlp_w_down[layer])
    return rmsnorm(x, final_norm)
```

```python
import functools

import jax
import jax.numpy as jnp
from jax import lax
from jax.experimental import pallas as pl
from jax.experimental.pallas import tpu as pltpu

F32 = jnp.float32
BF16 = jnp.bfloat16

CHUNK = 64
NORM_EPS = 1e-6
ROPE_BASE = 10000.0
RET_HEADS = 8
GDN_HEADS = 16
GLA_HEADS = 4
GLA_GATE_RANK = 16
GLA_TAU = 16.0
LRU_BLOCK = 128
LRU_C = 8.0
CONV_WIDTH = 4

LANES = 128
VMEM_LIMIT = 56 * 1024 * 1024


def _cparams(sem):
    return pltpu.CompilerParams(dimension_semantics=sem, vmem_limit_bytes=VMEM_LIMIT)


def _rms(x, g):
    return x * lax.rsqrt(jnp.mean(x * x, axis=-1, keepdims=True) + NORM_EPS) * g


def _softplus(x):
    return jnp.maximum(x, 0.0) + jnp.log1p(jnp.exp(-jnp.abs(x)))


def _neg_expm1(z):
    u = jnp.exp(z)
    um1 = u - 1.0
    return jnp.where(um1 == 0.0, -z, -(um1 * z) / jnp.log(u))


def _bdot(a, b):
    return jnp.dot(a.astype(BF16), b.astype(BF16), preferred_element_type=F32)


def _bdot_nt(a, b):
    return lax.dot_general(a.astype(BF16), b.astype(BF16), (((1,), (1,)), ((), ())),
                           preferred_element_type=F32)


def _bdot_tn(a, b):
    return lax.dot_general(a.astype(BF16), b.astype(BF16), (((0,), (0,)), ((), ())),
                           preferred_element_type=F32)


def _fdot(a, b):
    return jnp.dot(a, b, precision=lax.Precision.HIGHEST, preferred_element_type=F32)


def _tril_ones(n):
    r = lax.broadcasted_iota(jnp.int32, (n, n), 0)
    c = lax.broadcasted_iota(jnp.int32, (n, n), 1)
    return (r >= c).astype(F32)


def _norm_kernel(x_ref, g_ref, o_ref):
    o_ref[...] = _rms(x_ref[...], g_ref[...]).astype(o_ref.dtype)


def _rmsnorm(x, g, out_dtype, tm=512):
    m, d = x.shape
    return pl.pallas_call(
        _norm_kernel, grid=(m // tm,),
        in_specs=[pl.BlockSpec((tm, d), lambda i: (i, 0)), pl.BlockSpec((1, d), lambda i: (0, 0))],
        out_specs=pl.BlockSpec((tm, d), lambda i: (i, 0)),
        out_shape=jax.ShapeDtypeStruct((m, d), out_dtype),
        compiler_params=_cparams(("parallel",)), name="rmsnorm")(x, g.reshape(1, d))


def _mm_kernel(a_ref, w_ref, o_ref):
    o_ref[...] = jnp.dot(a_ref[...], w_ref[...], preferred_element_type=F32).astype(o_ref.dtype)


def _matmul(a, w, out_dtype, tm, tn, name):
    m, k = a.shape
    n = w.shape[1]
    return pl.pallas_call(
        _mm_kernel, grid=(m // tm, n // tn),
        in_specs=[pl.BlockSpec((tm, k), lambda i, j: (i, 0)), pl.BlockSpec((k, tn), lambda i, j: (0, j))],
        out_specs=pl.BlockSpec((tm, tn), lambda i, j: (i, j)),
        out_shape=jax.ShapeDtypeStruct((m, n), out_dtype),
        compiler_params=_cparams(("parallel", "parallel")), name=name)(a, w)


def _proj_res_norm_kernel(a_ref, w_ref, x_ref, g_ref, xo_ref, ho_ref):
    x = x_ref[...] + jnp.dot(a_ref[...], w_ref[...], preferred_element_type=F32)
    xo_ref[...] = x
    ho_ref[...] = _rms(x, g_ref[...]).astype(ho_ref.dtype)


def _proj_res_norm(a, w, x, g, tm=256):
    m, k = a.shape
    d = w.shape[1]
    return pl.pallas_call(
        _proj_res_norm_kernel, grid=(m // tm,),
        in_specs=[pl.BlockSpec((tm, k), lambda i: (i, 0)),
                  pl.BlockSpec((k, d), lambda i: (0, 0), pipeline_mode=pl.Buffered(1)),
                  pl.BlockSpec((tm, d), lambda i: (i, 0)),
                  pl.BlockSpec((1, d), lambda i: (0, 0))],
        out_specs=[pl.BlockSpec((tm, d), lambda i: (i, 0)), pl.BlockSpec((tm, d), lambda i: (i, 0))],
        out_shape=[jax.ShapeDtypeStruct((m, d), F32), jax.ShapeDtypeStruct((m, d), BF16)],
        compiler_params=_cparams(("parallel",)), name="proj_res_norm")(a, w, x, g.reshape(1, d))


def _mlp_kernel(h_ref, wu_ref, wd_ref, x_ref, g_ref, *rest, write_x):
    if write_x:
        xo_ref, ho_ref, acc_ref = rest
    else:
        ho_ref, acc_ref = rest
    f = pl.program_id(1)
    u = jnp.dot(h_ref[...], wu_ref[...], preferred_element_type=F32)
    u = jnp.square(jnp.maximum(u, 0.0)).astype(BF16)
    part = jnp.dot(u, wd_ref[...], preferred_element_type=F32)

    @pl.when(f == 0)
    def _():
        acc_ref[...] = x_ref[...] + part

    @pl.when(f > 0)
    def _():
        acc_ref[...] += part

    @pl.when(f == pl.num_programs(1) - 1)
    def _():
        x = acc_ref[...]
        if write_x:
            xo_ref[...] = x
        ho_ref[...] = _rms(x, g_ref[...]).astype(ho_ref.dtype)


def _mlp(h, wu, wd, x, g, write_x, h_dtype, tm=512, tf=512):
    m, d = h.shape
    ff = wu.shape[1]
    row = pl.BlockSpec((tm, d), lambda i, f: (i, 0))
    out_specs = [row]
    out_shape = [jax.ShapeDtypeStruct((m, d), h_dtype)]
    if write_x:
        out_specs = [row, row]
        out_shape = [jax.ShapeDtypeStruct((m, d), F32)] + out_shape
    return pl.pallas_call(
        functools.partial(_mlp_kernel, write_x=write_x), grid=(m // tm, ff // tf),
        in_specs=[row,
                  pl.BlockSpec((d, tf), lambda i, f: (0, f)),
                  pl.BlockSpec((tf, d), lambda i, f: (f, 0)),
                  row,
                  pl.BlockSpec((1, d), lambda i, f: (0, 0))],
        out_specs=out_specs, out_shape=out_shape,
        scratch_shapes=[pltpu.VMEM((tm, d), F32)],
        compiler_params=_cparams(("parallel", "arbitrary")), name="mlp")(h, wu, wd, x, g.reshape(1, d))


def _rotary(x, c, s):
    half = x.shape[-1] // 2
    x1, x2 = x[:, :half], x[:, half:]
    return jnp.concatenate([x1 * c - x2 * s, x1 * s + x2 * c], axis=-1)


def _ret_kernel(q_ref, k_ref, v_ref, g_ref, cos_ref, sin_ref, dmat_ref, qd_ref, kd_ref, cd_ref,
                gain_ref, o_ref, state_ref):
    seq = q_ref.shape[1]
    dk = q_ref.shape[2]
    state_ref[...] = jnp.zeros_like(state_ref)
    dmat = dmat_ref[0]
    qd = qd_ref[0]
    kd = kd_ref[0]
    cd = cd_ref[0]
    gain = gain_ref[0]

    def body(n, carry):
        r = pl.ds(pl.multiple_of(n * CHUNK, CHUNK), CHUNK)
        c = cos_ref[r, :]
        s = sin_ref[r, :]
        q = _rotary(q_ref[0, r, :].astype(F32), c, s)
        k = _rotary(k_ref[0, r, :].astype(F32), c, s) * dk ** -0.5
        v = v_ref[0, r, :]
        scores = _bdot_nt(q, k) * dmat
        state = state_ref[...]
        o = _bdot(scores, v) + _bdot(q * qd, state)
        state_ref[...] = state * cd + _bdot_tn(k * kd, v)
        o = o - jnp.mean(o, axis=-1, keepdims=True)
        o = _rms(o, gain)
        g = g_ref[0, r, :].astype(F32)
        o_ref[0, r, :] = (o * (g * jax.nn.sigmoid(g))).astype(o_ref.dtype)
        return carry

    lax.fori_loop(0, seq // CHUNK, body, 0)


def _retention_core(proj, gn_gain, batch, seq):
    h = RET_HEADS
    dv = gn_gain.shape[-1]
    dk = dv // 2
    p3 = proj.reshape(batch, seq, -1)
    inv = ROPE_BASE ** (-jnp.arange(0, dk, 2, dtype=F32) / dk)
    ang = jnp.arange(seq, dtype=F32)[:, None] * inv[None, :]
    cos, sin = jnp.cos(ang), jnp.sin(ang)
    log_gamma = jnp.log1p(-jnp.exp2(-5.0 - jnp.arange(h, dtype=F32)))
    pos = jnp.arange(CHUNK, dtype=F32)
    dist = jnp.abs(pos[:, None] - pos[None, :])
    dmat = jnp.exp(log_gamma[:, None, None] * dist)
    qd = jnp.exp(log_gamma[:, None] * (pos + 1.0))[:, :, None]
    kd = jnp.exp(log_gamma[:, None] * (CHUNK - 1.0 - pos))[:, :, None]
    cd = jnp.exp(log_gamma * CHUNK)[:, None, None]
    per_head = lambda shape: pl.BlockSpec((1,) + shape, lambda b, i: (i, 0, 0))
    return pl.pallas_call(
        _ret_kernel, grid=(batch, h),
        in_specs=[pl.BlockSpec((1, seq, dk), lambda b, i: (b, 0, i)),
                  pl.BlockSpec((1, seq, dk), lambda b, i: (b, 0, h + i)),
                  pl.BlockSpec((1, seq, dv), lambda b, i: (b, 0, h + i)),
                  pl.BlockSpec((1, seq, dv), lambda b, i: (b, 0, 2 * h + i)),
                  pl.BlockSpec((seq, dk // 2), lambda b, i: (0, 0)),
                  pl.BlockSpec((seq, dk // 2), lambda b, i: (0, 0)),
                  per_head((CHUNK, CHUNK)), per_head((CHUNK, 1)), per_head((CHUNK, 1)),
                  per_head((1, 1)), per_head((1, dv))],
        out_specs=pl.BlockSpec((1, seq, dv), lambda b, i: (b, 0, i)),
        out_shape=jax.ShapeDtypeStruct((batch, seq, h * dv), BF16),
        scratch_shapes=[pltpu.VMEM((dk, dv), F32)],
        compiler_params=_cparams(("parallel", "parallel")), name="retention_core",
    )(p3, p3, p3, p3, cos, sin, dmat, qd, kd, cd, gn_gain.reshape(h, 1, dv))


def _gla_kernel(q_ref, k_ref, v_ref, r_ref, gl_ref, wg_ref, gb_ref, gain_ref, o_ref, state_ref):
    seq = q_ref.shape[1]
    dk = q_ref.shape[2]
    state_ref[...] = jnp.zeros_like(state_ref)
    tri = _tril_ones(CHUNK)
    row = lax.broadcasted_iota(jnp.int32, (CHUNK, CHUNK), 0)
    col = lax.broadcasted_iota(jnp.int32, (CHUNK, CHUNK), 1)
    wg = wg_ref[...]
    gb = gb_ref[...]
    gain = gain_ref[0]

    def body(n, carry):
        r = pl.ds(pl.multiple_of(n * CHUNK, CHUNK), CHUNK)
        logit = _fdot(gl_ref[0, r, :], wg) + gb
        log_alpha = -_softplus(-logit) / GLA_TAU
        cum = _fdot(tri, log_alpha)
        ref = cum[CHUNK // 2 - 1:CHUNK // 2, :]
        last = cum[CHUNK - 1:CHUNK, :]
        fwd, bwd = jnp.exp(cum - ref), jnp.exp(ref - cum)
        q = q_ref[0, r, :].astype(F32) * dk ** -0.5
        k = k_ref[0, r, :].astype(F32)
        v = v_ref[0, r, :]
        s_lo = _bdot_nt(q * fwd, k * bwd)
        s_up = _bdot_nt(q * bwd, k * fwd)
        scores = jnp.where(row >= col, s_lo, s_up)
        state = state_ref[...]
        o = _bdot(scores, v) + _bdot_nt(q * jnp.exp(cum), state)
        state_ref[...] = state * jnp.exp(last) + _bdot_tn(v, k * jnp.exp(last - cum))
        o = _rms(o, gain)
        g = r_ref[0, r, :].astype(F32)
        o_ref[0, r, :] = (o * (g * jax.nn.sigmoid(g))).astype(o_ref.dtype)
        return carry

    lax.fori_loop(0, seq // CHUNK, body, 0)


def _gla_core(proj, gate_low, w_gate_up, gate_bias, norm_gain, batch, seq):
    h = GLA_HEADS
    dv = norm_gain.shape[-1]
    dk = dv // 2
    p3 = proj.reshape(batch, seq, -1)
    gl3 = gate_low.reshape(batch, seq, LANES)
    wg = jnp.zeros((LANES, h * dk), F32).at[:GLA_GATE_RANK].set(w_gate_up)
    return pl.pallas_call(
        _gla_kernel, grid=(batch, h),
        in_specs=[pl.BlockSpec((1, seq, dk), lambda b, i: (b, 0, i)),
                  pl.BlockSpec((1, seq, dk), lambda b, i: (b, 0, h + i)),
                  pl.BlockSpec((1, seq, dv), lambda b, i: (b, 0, h + i)),
                  pl.BlockSpec((1, seq, dv), lambda b, i: (b, 0, 2 * h + i)),
                  pl.BlockSpec((1, seq, LANES), lambda b, i: (b, 0, 0)),
                  pl.BlockSpec((LANES, dk), lambda b, i: (0, i)),
                  pl.BlockSpec((1, dk), lambda b, i: (0, i)),
                  pl.BlockSpec((1, 1, dv), lambda b, i: (i, 0, 0))],
        out_specs=pl.BlockSpec((1, seq, dv), lambda b, i: (b, 0, i)),
        out_shape=jax.ShapeDtypeStruct((batch, seq, h * dv), BF16),
        scratch_shapes=[pltpu.VMEM((dv, dk), F32)],
        compiler_params=_cparams(("parallel", "parallel")), name="gla_core",
    )(p3, p3, p3, p3, gl3, wg, gate_bias.reshape(1, h * dk), norm_gain.reshape(h, 1, dv))


def _causal_conv(x, w):
    width = w.shape[0]
    row = lax.broadcasted_iota(jnp.int32, x.shape, 0)
    out = x * w[width - 1:width, :]
    for sh in range(1, width):
        xs = jnp.where(row >= sh, pltpu.roll(x, sh, axis=0), 0.0)
        out = out + xs * w[width - 1 - sh:width - sh, :]
    return out


def _silu(x):
    return x * jax.nn.sigmoid(x)


def _l2norm(x):
    return x * lax.rsqrt(jnp.sum(x * x, axis=-1, keepdims=True) + NORM_EPS)


def _gdn_gate_kernel(bl_ref, al_ref, alog_ref, dt_ref, beta_ref, cum_ref):
    beta_ref[0] = jax.nn.sigmoid(bl_ref[0])
    log_alpha = -jnp.exp(alog_ref[...]) * _softplus(al_ref[0] + dt_ref[...])
    cum_ref[0] = _fdot(_tril_ones(CHUNK), log_alpha)


def _gdn_gates(gates, a_log, dt_bias, batch, seq):
    h = GDN_HEADS
    n = seq // CHUNK
    g4 = gates.reshape(batch, n, CHUNK, LANES)

    def to_pos_major(x):
        return x.transpose(0, 2, 3, 1).reshape(batch, CHUNK, h * n)

    bl = to_pos_major(g4[..., :h])
    al = to_pos_major(g4[..., h:2 * h])
    spec = pl.BlockSpec((1, CHUNK, h * n), lambda b: (b, 0, 0))
    tab = pl.BlockSpec((1, h * n), lambda b: (0, 0))
    beta, cum = pl.pallas_call(
        _gdn_gate_kernel, grid=(batch,),
        in_specs=[spec, spec, tab, tab], out_specs=[spec, spec],
        out_shape=[jax.ShapeDtypeStruct((batch, CHUNK, h * n), F32)] * 2,
        compiler_params=_cparams(("parallel",)), name="gdn_gates",
    )(bl, al, jnp.repeat(a_log.astype(F32), n).reshape(1, h * n),
      jnp.repeat(dt_bias.astype(F32), n).reshape(1, h * n))

    def to_head_major(x):
        return x.reshape(batch, CHUNK, h, n).transpose(0, 2, 3, 1)

    return to_head_major(beta), to_head_major(cum)


def _unit_lower_inverse_minus_eye(a, row, col):
    def blk(size):
        return (row // size) == (col // size)

    d = jnp.where(blk(8), a, 0.0)
    d2 = _bdot(d, d)
    d3 = _bdot(d2, d)
    d4 = _bdot(d2, d2)
    q = d2 - d - d3
    y = q + d4 + _bdot(q, d4)
    for size in (16, 32, 64):
        e = jnp.where(blk(size) & jnp.logical_not(blk(size // 2)), a, 0.0)
        t = e + _bdot(y, e)
        y = y - t - _bdot(t, y)
    return y


def _gdn_kernel(q_ref, k_ref, v_ref, z_ref, wq_ref, wk_ref, wv_ref, betac_ref, cumc_ref, cumr_ref,
                gain_ref, o_ref, q_s, k_s, v_s, u_s, w_s, ke_s):
    seq = q_ref.shape[1]
    dk = q_ref.shape[2]
    n_chunks = seq // CHUNK
    q_s[...] = (_l2norm(_silu(_causal_conv(q_ref[0].astype(F32), wq_ref[...]))) * dk ** -0.5).astype(q_s.dtype)
    k_s[...] = _l2norm(_silu(_causal_conv(k_ref[0].astype(F32), wk_ref[...])))
    v_s[...] = _silu(_causal_conv(v_ref[0].astype(F32), wv_ref[...]))
    row = lax.broadcasted_iota(jnp.int32, (CHUNK, CHUNK), 0)
    col = lax.broadcasted_iota(jnp.int32, (CHUNK, CHUNK), 1)
    strict = row > col
    gain = gain_ref[...]

    def solve_chunk(n):
        r = pl.ds(pl.multiple_of(n * CHUNK, CHUNK), CHUNK)
        k = k_s[r, :]
        v = v_s[r, :]
        cum_c = cumc_ref[0, 0, n]
        cum_r = cumr_ref[0, 0, n]
        beta = betac_ref[0, 0, n]
        rel = jnp.where(strict, jnp.exp(jnp.where(strict, cum_c - cum_r, 0.0)), 0.0)
        a = beta * rel * _bdot_nt(k, k)
        y = _unit_lower_inverse_minus_eye(a, row, col)
        rhs = jnp.concatenate([beta * v, (beta * jnp.exp(cum_c)) * k], axis=-1)
        sol = rhs + _bdot(y, rhs)
        u_s[r, :] = sol[:, :dk]
        w_s[r, :] = sol[:, dk:].astype(w_s.dtype)
        ke_s[r, :] = (k * jnp.exp(cum_c[CHUNK - 1:CHUNK, :] - cum_c)).astype(ke_s.dtype)

    unroll = 4

    def solve_body(i, carry):
        for j in range(unroll):
            solve_chunk(i * unroll + j)
        return carry

    lax.fori_loop(0, n_chunks // unroll, solve_body, 0)

    def state_body(n, state):
        r = pl.ds(pl.multiple_of(n * CHUNK, CHUNK), CHUNK)
        dec = jnp.exp(cumc_ref[0, 0, n][CHUNK - 1:CHUNK, :])
        delta = u_s[r, :] - _bdot(w_s[r, :], state)
        state = state * dec + _bdot_tn(ke_s[r, :], delta)
        o = _rms(_bdot(q_s[r, :], state), gain)
        o_ref[0, r, :] = (o * _silu(z_ref[0, r, :].astype(F32))).astype(o_ref.dtype)
        return state

    lax.fori_loop(0, n_chunks, state_body, jnp.zeros((dk, dk), F32))


def _gdn_core(proj, gates, conv_w, a_log, dt_bias, norm_gain, batch, seq):
    h = GDN_HEADS
    dk = norm_gain.shape[-1]
    n = seq // CHUNK
    p3 = proj.reshape(batch, seq, -1)
    beta, cum = _gdn_gates(gates, a_log, dt_bias, batch, seq)
    col_spec = pl.BlockSpec((1, 1, n, CHUNK, 1), lambda b, i: (b, i, 0, 0, 0))
    row_spec = pl.BlockSpec((1, 1, n, 1, CHUNK), lambda b, i: (b, i, 0, 0, 0))
    head = lambda off: pl.BlockSpec((1, seq, dk), lambda b, i: (b, 0, off + i))
    convw = lambda off: pl.BlockSpec((CONV_WIDTH, dk), lambda b, i: (0, off + i))
    return pl.pallas_call(
        _gdn_kernel, grid=(batch, h),
        in_specs=[head(0), head(h), head(2 * h), head(3 * h),
                  convw(0), convw(h), convw(2 * h),
                  col_spec, col_spec, row_spec,
                  pl.BlockSpec((1, dk), lambda b, i: (0, 0))],
        out_specs=pl.BlockSpec((1, seq, dk), lambda b, i: (b, 0, i)),
        out_shape=jax.ShapeDtypeStruct((batch, seq, h * dk), BF16),
        scratch_shapes=[pltpu.VMEM((seq, dk), BF16), pltpu.VMEM((seq, dk), F32), pltpu.VMEM((seq, dk), F32),
                        pltpu.VMEM((seq, dk), F32), pltpu.VMEM((seq, dk), BF16), pltpu.VMEM((seq, dk), BF16)],
        compiler_params=_cparams(("parallel", "parallel")), name="gdn_core",
    )(p3, p3, p3, p3, conv_w, conv_w, conv_w,
      beta[..., None], cum[..., None], cum[:, :, :, None, :], norm_gain.reshape(1, dk))


_SCAN_ROWS = 8


def _lru_kernel(x_ref, y_ref, cw_ref, cb_ref, w_ref, br_ref, bi_ref, lam_ref, o_ref, a_s, u_s):
    seq = x_ref.shape[1]
    width = x_ref.shape[2]
    x = _causal_conv(x_ref[0].astype(F32), cw_ref[...]) + cb_ref[...]
    gates = _bdot(x, w_ref[0])
    r = jax.nn.sigmoid(gates[:, :width] + br_ref[...])
    i = jax.nn.sigmoid(gates[:, width:] + bi_ref[...])
    log_a = (-LRU_C * _softplus(-lam_ref[...])) * r
    a_s[...] = jnp.exp(log_a)
    u_s[...] = jnp.sqrt(_neg_expm1(2.0 * log_a)) * (i * x)
    row = lax.broadcasted_iota(jnp.int32, (_SCAN_ROWS, width), 0)

    def scan_block(t, h_prev):
        rows = pl.ds(pl.multiple_of(t * _SCAN_ROWS, _SCAN_ROWS), _SCAN_ROWS)
        a = a_s[rows, :]
        u = u_s[rows, :]
        for d in (1, 2, 4):
            m = row >= d
            u = jnp.where(m, a * pltpu.roll(u, d, axis=0) + u, u)
            a = jnp.where(m, a * pltpu.roll(a, d, axis=0), a)
        h = u + a * h_prev
        y = jax.nn.gelu(y_ref[0, rows, :].astype(F32))
        o_ref[0, rows, :] = (h * y).astype(o_ref.dtype)
        return jnp.broadcast_to(h[_SCAN_ROWS - 1:_SCAN_ROWS, :], h.shape)

    unroll = 8

    def body(j, h_prev):
        for s in range(unroll):
            h_prev = scan_block(j * unroll + s, h_prev)
        return h_prev

    lax.fori_loop(0, seq // (_SCAN_ROWS * unroll), body, jnp.zeros((_SCAN_ROWS, width), F32))


def _lru_core(proj, conv_w, conv_b, w_r, b_r, w_i, b_i, lam, batch, seq):
    width = conv_w.shape[-1]
    nb = width // LRU_BLOCK
    p3 = proj.reshape(batch, seq, -1)
    w_ri = jnp.concatenate([w_r, w_i], axis=-1).astype(BF16)
    vec = lambda: pl.BlockSpec((1, LRU_BLOCK), lambda b, j: (0, j))
    return pl.pallas_call(
        _lru_kernel, grid=(batch, nb),
        in_specs=[pl.BlockSpec((1, seq, LRU_BLOCK), lambda b, j: (b, 0, j)),
                  pl.BlockSpec((1, seq, LRU_BLOCK), lambda b, j: (b, 0, nb + j)),
                  pl.BlockSpec((CONV_WIDTH, LRU_BLOCK), lambda b, j: (0, j)),
                  vec(),
                  pl.BlockSpec((1, LRU_BLOCK, 2 * LRU_BLOCK), lambda b, j: (j, 0, 0)),
                  vec(), vec(), vec()],
        out_specs=pl.BlockSpec((1, seq, LRU_BLOCK), lambda b, j: (b, 0, j)),
        out_shape=jax.ShapeDtypeStruct((batch, seq, width), BF16),
        scratch_shapes=[pltpu.VMEM((seq, LRU_BLOCK), F32), pltpu.VMEM((seq, LRU_BLOCK), F32)],
        compiler_params=_cparams(("parallel", "parallel")), name="lru_core",
    )(p3, p3, conv_w, conv_b.reshape(1, width), w_ri, b_r.reshape(1, width), b_i.reshape(1, width),
      lam.reshape(1, width))


def _pad_cols(w, n):
    return jnp.zeros((w.shape[0], n), w.dtype).at[:, :w.shape[1]].set(w)


def kernel(x, norm1, norm2, final_norm, ret_w_in, ret_gn_gain, ret_w_out, gdn_w_in, gdn_conv_w, gdn_a_log, gdn_dt_bias, gdn_norm_gain, gdn_w_out, gla_w_in, gla_w_gate_up, gla_gate_bias, gla_norm_gain, gla_w_out, lru_w_in, lru_conv_w, lru_conv_b, lru_w_rgate, lru_b_rgate, lru_w_igate, lru_b_igate, lru_lambda, lru_w_out, mlp_w_up, mlp_w_down):
    batch, seq, d = x.shape
    depth = norm1.shape[0]
    xf = x.reshape(batch * seq, d)
    hn = _rmsnorm(xf, norm1[0], BF16)
    for layer in range(depth):
        m, j = layer % 4, layer // 4
        if m == 0:
            proj = _matmul(hn, ret_w_in[j].astype(BF16), BF16, 1024, 1024, "ret_in")
            o = _retention_core(proj, ret_gn_gain[j], batch, seq)
            w_out = ret_w_out[j]
        elif m == 1:
            w_in = gdn_w_in[j]
            n_main = GDN_HEADS * 4 * gdn_norm_gain.shape[-1]
            proj = _matmul(hn, w_in[:, :n_main].astype(BF16), BF16, 1024, 1024, "gdn_in")
            gates = _matmul(hn, _pad_cols(w_in[:, n_main:], LANES).astype(BF16), F32, 1024, LANES, "gdn_gate_in")
            o = _gdn_core(proj, gates, gdn_conv_w[j], gdn_a_log[j], gdn_dt_bias[j], gdn_norm_gain[j], batch, seq)
            w_out = gdn_w_out[j]
        elif m == 2:
            w_in = gla_w_in[j]
            n_main = w_in.shape[1] - GLA_GATE_RANK
            proj = _matmul(hn, w_in[:, :n_main].astype(BF16), BF16, 1024, 1024, "gla_in")
            gate_low = _matmul(hn, _pad_cols(w_in[:, n_main:], LANES).astype(BF16), F32, 1024, LANES, "gla_gate_in")
            o = _gla_core(proj, gate_low, gla_w_gate_up[j], gla_gate_bias[j], gla_norm_gain[j], batch, seq)
            w_out = gla_w_out[j]
        else:
            proj = _matmul(hn, lru_w_in[j].astype(BF16), BF16, 1024, 1024, "lru_in")
            o = _lru_core(proj, lru_conv_w[j], lru_conv_b[j], lru_w_rgate[j], lru_b_rgate[j].reshape(-1),
                          lru_w_igate[j], lru_b_igate[j].reshape(-1), lru_lambda[j], batch, seq)
            w_out = lru_w_out[j]
        xf, hn = _proj_res_norm(o.reshape(batch * seq, -1), w_out.astype(BF16), xf, norm2[layer])
        wu, wd = mlp_w_up[layer].astype(BF16), mlp_w_down[layer].astype(BF16)
        if layer + 1 < depth:
            xf, hn = _mlp(hn, wu, wd, xf, norm1[layer + 1], True, BF16)
        else:
            (out,) = _mlp(hn, wu, wd, xf, final_norm, False, x.dtype)
    return out.reshape(batch, seq, d)
```

```python
import functools

import jax
import jax.numpy as jnp
from jax import lax
from jax.experimental import pallas as pl
from jax.experimental.pallas import tpu as pltpu

F32 = jnp.float32
BF16 = jnp.bfloat16

CHUNK = 64
NORM_EPS = 1e-6
ROPE_BASE = 10000.0
RET_HEADS = 8
GDN_HEADS = 16
GLA_HEADS = 4
GLA_GATE_RANK = 16
GLA_TAU = 16.0
LRU_BLOCK = 128
LRU_C = 8.0
CONV_WIDTH = 4

LANES = 128
SUBLANES = 8
VMEM_LIMIT = 56 * 1024 * 1024

IN_PROJ_TILE = (1024, 1024)
OUT_PROJ_ROWS = 256
MLP_TILE = (512, 1024)
RET_GROUP = 2
GLA_GROUP = 2
GDN_GROUP = 4
GDN_SOLVE_BATCH = 8


def _cparams(sem):
    return pltpu.CompilerParams(dimension_semantics=sem, vmem_limit_bytes=VMEM_LIMIT)


def _rms(x, g):
    return x * lax.rsqrt(jnp.mean(x * x, axis=-1, keepdims=True) + NORM_EPS) * g


def _softplus(x):
    return jnp.maximum(x, 0.0) + jnp.log1p(jnp.exp(-jnp.abs(x)))


def _silu(x):
    return x * jax.nn.sigmoid(x)


def _neg_expm1(z):
    u = jnp.exp(z)
    um1 = u - 1.0
    return jnp.where(um1 == 0.0, -z, -(um1 * z) / jnp.log(u))


def _bdot(a, b):
    return jnp.dot(a.astype(BF16), b.astype(BF16), preferred_element_type=F32)


def _bdot_nt(a, b):
    return lax.dot_general(a.astype(BF16), b.astype(BF16), (((1,), (1,)), ((), ())),
                           preferred_element_type=F32)


def _bdot_tn(a, b):
    return lax.dot_general(a.astype(BF16), b.astype(BF16), (((0,), (0,)), ((), ())),
                           preferred_element_type=F32)


def _bmm(a, b):
    return jnp.einsum("uij,ujk->uik", a.astype(BF16), b.astype(BF16), preferred_element_type=F32)


def _bmm_nt(a, b):
    return jnp.einsum("uid,ujd->uij", a.astype(BF16), b.astype(BF16), preferred_element_type=F32)


def _fdot(a, b):
    return jnp.dot(a, b, precision=lax.Precision.HIGHEST, preferred_element_type=F32)


def _tril_ones(n):
    r = lax.broadcasted_iota(jnp.int32, (n, n), 0)
    c = lax.broadcasted_iota(jnp.int32, (n, n), 1)
    return (r >= c).astype(F32)


def _norm_kernel(x_ref, g_ref, o_ref):
    o_ref[...] = _rms(x_ref[...], g_ref[...]).astype(o_ref.dtype)


def _rmsnorm(x, g, out_dtype, tm=512):
    m, d = x.shape
    return pl.pallas_call(
        _norm_kernel, grid=(m // tm,),
        in_specs=[pl.BlockSpec((tm, d), lambda i: (i, 0)), pl.BlockSpec((1, d), lambda i: (0, 0))],
        out_specs=pl.BlockSpec((tm, d), lambda i: (i, 0)),
        out_shape=jax.ShapeDtypeStruct((m, d), out_dtype),
        compiler_params=_cparams(("parallel",)), name="rmsnorm")(x, g.reshape(1, d))


def _in_proj_kernel(a_ref, w_ref, o_ref, wb_ref):
    @pl.when(pl.program_id(1) == 0)
    def _():
        wb_ref[...] = w_ref[...].astype(wb_ref.dtype)

    o_ref[...] = jnp.dot(a_ref[...], wb_ref[...], preferred_element_type=F32).astype(o_ref.dtype)


def _in_proj(a, w, n_cols, out_dtype, name):
    m, k = a.shape
    tm, tn = IN_PROJ_TILE
    tm, tn = min(tm, m), min(tn, n_cols)
    return pl.pallas_call(
        _in_proj_kernel, grid=(n_cols // tn, m // tm),
        in_specs=[pl.BlockSpec((tm, k), lambda j, i: (i, 0)), pl.BlockSpec((k, tn), lambda j, i: (0, j))],
        out_specs=pl.BlockSpec((tm, tn), lambda j, i: (i, j)),
        out_shape=jax.ShapeDtypeStruct((m, n_cols), out_dtype),
        scratch_shapes=[pltpu.VMEM((k, tn), BF16)],
        compiler_params=_cparams(("parallel", "arbitrary")), name=name)(a, w)


def _mm_kernel(a_ref, w_ref, o_ref):
    o_ref[...] = jnp.dot(a_ref[...], w_ref[...], preferred_element_type=F32).astype(o_ref.dtype)


def _gate_proj(a, w, name, tm=1024):
    m, k = a.shape
    wp = jnp.zeros((k, LANES), BF16).at[:, :w.shape[1]].set(w.astype(BF16))
    tm = min(tm, m)
    return pl.pallas_call(
        _mm_kernel, grid=(m // tm,),
        in_specs=[pl.BlockSpec((tm, k), lambda i: (i, 0)), pl.BlockSpec((k, LANES), lambda i: (0, 0))],
        out_specs=pl.BlockSpec((tm, LANES), lambda i: (i, 0)),
        out_shape=jax.ShapeDtypeStruct((m, LANES), F32),
        compiler_params=_cparams(("parallel",)), name=name)(a, wp)


def _proj_res_norm_kernel(a_ref, w_ref, x_ref, g_ref, xo_ref, ho_ref):
    x = x_ref[...] + jnp.dot(a_ref[...], w_ref[...], preferred_element_type=F32)
    xo_ref[...] = x
    ho_ref[...] = _rms(x, g_ref[...]).astype(ho_ref.dtype)


def _proj_res_norm(a, w, x, g):
    m, k = a.shape
    d = w.shape[1]
    tm = min(OUT_PROJ_ROWS, m)
    return pl.pallas_call(
        _proj_res_norm_kernel, grid=(m // tm,),
        in_specs=[pl.BlockSpec((tm, k), lambda i: (i, 0)),
                  pl.BlockSpec((k, d), lambda i: (0, 0), pipeline_mode=pl.Buffered(1)),
                  pl.BlockSpec((tm, d), lambda i: (i, 0)),
                  pl.BlockSpec((1, d), lambda i: (0, 0))],
        out_specs=[pl.BlockSpec((tm, d), lambda i: (i, 0)), pl.BlockSpec((tm, d), lambda i: (i, 0))],
        out_shape=[jax.ShapeDtypeStruct((m, d), F32), jax.ShapeDtypeStruct((m, d), BF16)],
        compiler_params=_cparams(("parallel",)), name="proj_res_norm")(a, w, x, g.reshape(1, d))


def _mlp_kernel(h_ref, wu_ref, wd_ref, x_ref, g_ref, *rest, write_x):
    if write_x:
        xo_ref, ho_ref, acc_ref = rest
    else:
        ho_ref, acc_ref = rest
    f = pl.program_id(1)

    @pl.when(f == 0)
    def _():
        acc_ref[...] = x_ref[...]

    u = jnp.dot(h_ref[...], wu_ref[...], preferred_element_type=F32)
    u = jnp.square(jnp.maximum(u, 0.0)).astype(BF16)
    acc_ref[...] += jnp.dot(u, wd_ref[...], preferred_element_type=F32)

    @pl.when(f == pl.num_programs(1) - 1)
    def _():
        x = acc_ref[...]
        if write_x:
            xo_ref[...] = x
        ho_ref[...] = _rms(x, g_ref[...]).astype(ho_ref.dtype)


def _mlp(h, wu, wd, x, g, write_x, h_dtype):
    m, d = h.shape
    ff = wu.shape[1]
    tm, tf = MLP_TILE
    tm, tf = min(tm, m), min(tf, ff)
    row = pl.BlockSpec((tm, d), lambda i, f: (i, 0))
    out_specs = [row]
    out_shape = [jax.ShapeDtypeStruct((m, d), h_dtype)]
    if write_x:
        out_specs = [row, row]
        out_shape = [jax.ShapeDtypeStruct((m, d), F32)] + out_shape
    return pl.pallas_call(
        functools.partial(_mlp_kernel, write_x=write_x), grid=(m // tm, ff // tf),
        in_specs=[row,
                  pl.BlockSpec((d, tf), lambda i, f: (0, f)),
                  pl.BlockSpec((tf, d), lambda i, f: (f, 0)),
                  row,
                  pl.BlockSpec((1, d), lambda i, f: (0, 0))],
        out_specs=out_specs, out_shape=out_shape,
        scratch_shapes=[pltpu.VMEM((tm, d), F32)],
        compiler_params=_cparams(("parallel", "arbitrary")), name="mlp")(h, wu, wd, x, g.reshape(1, d))


def _rotary(x, c, s):
    half = x.shape[-1] // 2
    x1, x2 = x[:, :half], x[:, half:]
    return jnp.concatenate([x1 * c - x2 * s, x1 * s + x2 * c], axis=-1)


def _ret_kernel(q_ref, k_ref, v_ref, g_ref, cos_ref, sin_ref, dmat_ref, qd_ref, kd_ref, cd_ref,
                gain_ref, o_ref, state_ref):
    seq = q_ref.shape[1]
    groups = state_ref.shape[0]
    dk = q_ref.shape[2] // groups
    dv = v_ref.shape[2] // groups
    state_ref[...] = jnp.zeros_like(state_ref)

    def body(n, carry):
        r = pl.ds(pl.multiple_of(n * CHUNK, CHUNK), CHUNK)
        c = cos_ref[r, :]
        s = sin_ref[r, :]
        for h in range(groups):
            kc = slice(h * dk, (h + 1) * dk)
            vc = slice(h * dv, (h + 1) * dv)
            q = _rotary(q_ref[0, r, kc].astype(F32), c, s)
            k = _rotary(k_ref[0, r, kc].astype(F32), c, s) * dk ** -0.5
            v = v_ref[0, r, vc]
            scores = _bdot_nt(q, k) * dmat_ref[h]
            state = state_ref[h]
            o = _bdot(scores, v) + _bdot(q * qd_ref[h], state)
            state_ref[h] = state * cd_ref[h] + _bdot_tn(k * kd_ref[h], v)
            o = o - jnp.mean(o, axis=-1, keepdims=True)
            o = _rms(o, gain_ref[h])
            o_ref[0, r, vc] = (o * _silu(g_ref[0, r, vc].astype(F32))).astype(o_ref.dtype)
        return carry

    lax.fori_loop(0, seq // CHUNK, body, 0)


def _retention_core(proj, gn_gain, batch, seq):
    h = RET_HEADS
    grp = RET_GROUP
    ng = h // grp
    dv = gn_gain.shape[-1]
    dk = dv // 2
    p3 = proj.reshape(batch, seq, -1)
    inv = ROPE_BASE ** (-jnp.arange(0, dk, 2, dtype=F32) / dk)
    ang = jnp.arange(seq, dtype=F32)[:, None] * inv[None, :]
    cos, sin = jnp.cos(ang), jnp.sin(ang)
    log_gamma = jnp.log1p(-jnp.exp2(-5.0 - jnp.arange(h, dtype=F32)))
    pos = jnp.arange(CHUNK, dtype=F32)
    dist = jnp.abs(pos[:, None] - pos[None, :])
    dmat = jnp.exp(log_gamma[:, None, None] * dist)
    qd = jnp.exp(log_gamma[:, None] * (pos + 1.0))[:, :, None]
    kd = jnp.exp(log_gamma[:, None] * (CHUNK - 1.0 - pos))[:, :, None]
    cd = jnp.exp(log_gamma * CHUNK)[:, None, None]
    per_head = lambda shape: pl.BlockSpec((grp,) + shape, lambda b, i: (i, 0, 0))
    return pl.pallas_call(
        _ret_kernel, grid=(batch, ng),
        in_specs=[pl.BlockSpec((1, seq, grp * dk), lambda b, i: (b, 0, i)),
                  pl.BlockSpec((1, seq, grp * dk), lambda b, i: (b, 0, ng + i)),
                  pl.BlockSpec((1, seq, grp * dv), lambda b, i: (b, 0, ng + i)),
                  pl.BlockSpec((1, seq, grp * dv), lambda b, i: (b, 0, 2 * ng + i)),
                  pl.BlockSpec((seq, dk // 2), lambda b, i: (0, 0)),
                  pl.BlockSpec((seq, dk // 2), lambda b, i: (0, 0)),
                  per_head((CHUNK, CHUNK)), per_head((CHUNK, 1)), per_head((CHUNK, 1)),
                  per_head((1, 1)), per_head((1, dv))],
        out_specs=pl.BlockSpec((1, seq, grp * dv), lambda b, i: (b, 0, i)),
        out_shape=jax.ShapeDtypeStruct((batch, seq, h * dv), BF16),
        scratch_shapes=[pltpu.VMEM((grp, dk, dv), F32)],
        compiler_params=_cparams(("parallel", "parallel")), name="retention_core",
    )(p3, p3, p3, p3, cos, sin, dmat, qd, kd, cd, gn_gain.reshape(h, 1, dv))


def _gla_kernel(q_ref, k_ref, v_ref, r_ref, gl_ref, wg_ref, gb_ref, gain_ref, o_ref, state_ref):
    seq = q_ref.shape[1]
    groups = state_ref.shape[0]
    dk = q_ref.shape[2] // groups
    dv = v_ref.shape[2] // groups
    state_ref[...] = jnp.zeros_like(state_ref)
    tri = _tril_ones(CHUNK)
    row = lax.broadcasted_iota(jnp.int32, (CHUNK, CHUNK), 0)
    col = lax.broadcasted_iota(jnp.int32, (CHUNK, CHUNK), 1)

    def body(n, carry):
        r = pl.ds(pl.multiple_of(n * CHUNK, CHUNK), CHUNK)
        logit = _fdot(gl_ref[0, r, :], wg_ref[...]) + gb_ref[...]
        log_alpha = -_softplus(-logit) / GLA_TAU
        cum_all = _fdot(tri, log_alpha)
        for h in range(groups):
            kc = slice(h * dk, (h + 1) * dk)
            vc = slice(h * dv, (h + 1) * dv)
            cum = cum_all[:, kc]
            ref = cum[CHUNK // 2 - 1:CHUNK // 2, :]
            last = cum[CHUNK - 1:CHUNK, :]
            fwd, bwd = jnp.exp(cum - ref), jnp.exp(ref - cum)
            q = q_ref[0, r, kc].astype(F32) * dk ** -0.5
            k = k_ref[0, r, kc].astype(F32)
            v = v_ref[0, r, vc]
            s_lo = _bdot_nt(q * fwd, k * bwd)
            s_up = _bdot_nt(q * bwd, k * fwd)
            scores = jnp.where(row >= col, s_lo, s_up)
            state = state_ref[h]
            o = _bdot(scores, v) + _bdot_nt(q * jnp.exp(cum), state)
            state_ref[h] = state * jnp.exp(last) + _bdot_tn(v, k * jnp.exp(last - cum))
            o = _rms(o, gain_ref[h])
            o_ref[0, r, vc] = (o * _silu(r_ref[0, r, vc].astype(F32))).astype(o_ref.dtype)
        return carry

    lax.fori_loop(0, seq // CHUNK, body, 0)


def _gla_core(proj, gate_low, w_gate_up, gate_bias, norm_gain, batch, seq):
    h = GLA_HEADS
    grp = GLA_GROUP
    ng = h // grp
    dv = norm_gain.shape[-1]
    dk = dv // 2
    p3 = proj.reshape(batch, seq, -1)
    gl3 = gate_low.reshape(batch, seq, LANES)
    wg = jnp.zeros((LANES, h * dk), F32).at[:GLA_GATE_RANK].set(w_gate_up)
    return pl.pallas_call(
        _gla_kernel, grid=(batch, ng),
        in_specs=[pl.BlockSpec((1, seq, grp * dk), lambda b, i: (b, 0, i)),
                  pl.BlockSpec((1, seq, grp * dk), lambda b, i: (b, 0, ng + i)),
                  pl.BlockSpec((1, seq, grp * dv), lambda b, i: (b, 0, ng + i)),
                  pl.BlockSpec((1, seq, grp * dv), lambda b, i: (b, 0, 2 * ng + i)),
                  pl.BlockSpec((1, seq, LANES), lambda b, i: (b, 0, 0)),
                  pl.BlockSpec((LANES, grp * dk), lambda b, i: (0, i)),
                  pl.BlockSpec((1, grp * dk), lambda b, i: (0, i)),
                  pl.BlockSpec((grp, 1, dv), lambda b, i: (i, 0, 0))],
        out_specs=pl.BlockSpec((1, seq, grp * dv), lambda b, i: (b, 0, i)),
        out_shape=jax.ShapeDtypeStruct((batch, seq, h * dv), BF16),
        scratch_shapes=[pltpu.VMEM((grp, dv, dk), F32)],
        compiler_params=_cparams(("parallel", "parallel")), name="gla_core",
    )(p3, p3, p3, p3, gl3, wg, gate_bias.reshape(1, h * dk), norm_gain.reshape(h, 1, dv))


def _causal_conv(x, w):
    width = w.shape[0]
    xp = jnp.concatenate([jnp.zeros((SUBLANES, x.shape[1]), x.dtype), x], axis=0)
    out = x * w[width - 1:width, :]
    for sh in range(1, width):
        out = out + pltpu.roll(xp, sh, axis=0)[SUBLANES:] * w[width - 1 - sh:width - sh, :]
    return out


def _l2norm(x):
    return x * lax.rsqrt(jnp.sum(x * x, axis=-1, keepdims=True) + NORM_EPS)


def _gdn_gate_kernel(bl_ref, al_ref, alog_ref, dt_ref, beta_ref, cum_ref):
    beta_ref[0] = jax.nn.sigmoid(bl_ref[0])
    log_alpha = -jnp.exp(alog_ref[...]) * _softplus(al_ref[0] + dt_ref[...])
    cum_ref[0] = _fdot(_tril_ones(CHUNK), log_alpha)


def _gdn_gates(gates, a_log, dt_bias, batch, seq):
    h = GDN_HEADS
    n = seq // CHUNK
    g4 = gates.reshape(batch, n, CHUNK, LANES)

    def to_pos_major(x):
        return x.transpose(0, 2, 3, 1).reshape(batch, CHUNK, h * n)

    spec = pl.BlockSpec((1, CHUNK, h * n), lambda b: (b, 0, 0))
    tab = pl.BlockSpec((1, h * n), lambda b: (0, 0))
    return pl.pallas_call(
        _gdn_gate_kernel, grid=(batch,),
        in_specs=[spec, spec, tab, tab], out_specs=[spec, spec],
        out_shape=[jax.ShapeDtypeStruct((batch, CHUNK, h * n), F32)] * 2,
        compiler_params=_cparams(("parallel",)), name="gdn_gates",
    )(to_pos_major(g4[..., :h]), to_pos_major(g4[..., h:2 * h]),
      jnp.repeat(a_log.astype(F32), n).reshape(1, h * n),
      jnp.repeat(dt_bias.astype(F32), n).reshape(1, h * n))


def _unit_lower_inverse_minus_eye(a, row, col):
    def blk(size):
        return (row // size) == (col // size)

    d = jnp.where(blk(8), a, 0.0)
    d2 = _bmm(d, d)
    d3 = _bmm(d2, d)
    d4 = _bmm(d2, d2)
    q = d2 - d - d3
    y = q + d4 + _bmm(q, d4)
    for size in (16, 32, 64):
        e = jnp.where(blk(size) & jnp.logical_not(blk(size // 2)), a, 0.0)
        t = e + _bmm(y, e)
        y = y - t - _bmm(t, y)
    return y


def _lane_column(x, lane_ids, idx):
    return jnp.sum(jnp.where(lane_ids == idx, x, 0.0), axis=1, keepdims=True)


def _gdn_kernel(q_ref, k_ref, v_ref, z_ref, wq_ref, wk_ref, wv_ref, betat_ref, cumt_ref, cumr_ref,
                gain_ref, o_ref, q_s, k_s, v_s, u_s, w_s, ke_s, st_s):
    seq = q_ref.shape[1]
    groups = st_s.shape[0]
    dk = st_s.shape[1]
    n_chunks = seq // CHUNK
    ub = min(GDN_SOLVE_BATCH, n_chunks)
    row = lax.broadcasted_iota(jnp.int32, (CHUNK, CHUNK), 0)
    col = lax.broadcasted_iota(jnp.int32, (CHUNK, CHUNK), 1)
    strict = row > col
    betat = betat_ref[0]
    cumt = cumt_ref[0]
    lane_ids = lax.broadcasted_iota(jnp.int32, cumt.shape, 1)
    dec_all = jnp.exp(cumt[CHUNK - 1:CHUNK, :])
    gain = gain_ref[...]

    for g in range(groups):
        cs = slice(g * dk, (g + 1) * dk)
        qg = _l2norm(_silu(_causal_conv(q_ref[0, :, cs].astype(F32), wq_ref[:, cs])))
        q_s[g] = (qg * dk ** -0.5).astype(q_s.dtype)
        k_s[g] = _l2norm(_silu(_causal_conv(k_ref[0, :, cs].astype(F32), wk_ref[:, cs])))
        v_s[g] = _silu(_causal_conv(v_ref[0, :, cs].astype(F32), wv_ref[:, cs]))

    def solve(g, n0):
        r = pl.ds(pl.multiple_of(n0 * CHUNK, ub * CHUNK), ub * CHUNK)
        k = k_s[g, r, :].reshape(ub, CHUNK, dk)
        v = v_s[g, r, :].reshape(ub, CHUNK, dk)
        cum_c = jnp.stack([_lane_column(cumt, lane_ids, g * n_chunks + n0 + u) for u in range(ub)])
        beta = jnp.stack([_lane_column(betat, lane_ids, g * n_chunks + n0 + u) for u in range(ub)])
        cum_r = cumr_ref[0, g, pl.ds(n0, ub)]
        rel = jnp.where(strict, jnp.exp(jnp.where(strict, cum_c - cum_r, 0.0)), 0.0)
        a = beta * rel * _bmm_nt(k, k)
        y = _unit_lower_inverse_minus_eye(a, row, col)
        rhs = jnp.concatenate([beta * v, (beta * jnp.exp(cum_c)) * k], axis=-1)
        sol = rhs + _bmm(y, rhs)
        u_s[g, r, :] = sol[..., :dk].reshape(ub * CHUNK, dk)
        w_s[g, r, :] = sol[..., dk:].reshape(ub * CHUNK, dk).astype(w_s.dtype)
        ke = k * jnp.exp(cum_c[:, CHUNK - 1:CHUNK, :] - cum_c)
        ke_s[g, r, :] = ke.reshape(ub * CHUNK, dk).astype(ke_s.dtype)

    def solve_body(i, carry):
        for g in range(groups):
            solve(g, i * ub)
        return carry

    lax.fori_loop(0, n_chunks // ub, solve_body, 0)

    st_s[...] = jnp.zeros_like(st_s)

    def state_body(n, carry):
        r = pl.ds(pl.multiple_of(n * CHUNK, CHUNK), CHUNK)
        for g in range(groups):
            cs = slice(g * dk, (g + 1) * dk)
            dec = _lane_column(dec_all, lane_ids[:1], g * n_chunks + n)
            state = st_s[g]
            delta = u_s[g, r, :] - _bdot(w_s[g, r, :], state)
            state = state * dec + _bdot_tn(ke_s[g, r, :], delta)
            st_s[g] = state
            o = _rms(_bdot(q_s[g, r, :], state), gain)
            o_ref[0, r, cs] = (o * _silu(z_ref[0, r, cs].astype(F32))).astype(o_ref.dtype)
        return carry

    lax.fori_loop(0, n_chunks, state_body, 0)


def _gdn_core(proj, gates, conv_w, a_log, dt_bias, norm_gain, batch, seq):
    h = GDN_HEADS
    grp = GDN_GROUP
    ng = h // grp
    dk = norm_gain.shape[-1]
    n = seq // CHUNK
    p3 = proj.reshape(batch, seq, -1)
    beta_t, cum_t = _gdn_gates(gates, a_log, dt_bias, batch, seq)
    cum_r = cum_t.reshape(batch, CHUNK, h, n).transpose(0, 2, 3, 1)[:, :, :, None, :]
    head = lambda off: pl.BlockSpec((1, seq, grp * dk), lambda b, i: (b, 0, off + i))
    convw = lambda off: pl.BlockSpec((CONV_WIDTH, grp * dk), lambda b, i: (0, off + i))
    gate_spec = pl.BlockSpec((1, CHUNK, grp * n), lambda b, i: (b, 0, i))
    per_head = lambda dt: pltpu.VMEM((grp, seq, dk), dt)
    return pl.pallas_call(
        _gdn_kernel, grid=(batch, ng),
        in_specs=[head(0), head(ng), head(2 * ng), head(3 * ng),
                  convw(0), convw(ng), convw(2 * ng),
                  gate_spec, gate_spec,
                  pl.BlockSpec((1, grp, n, 1, CHUNK), lambda b, i: (b, i, 0, 0, 0)),
                  pl.BlockSpec((1, dk), lambda b, i: (0, 0))],
        out_specs=pl.BlockSpec((1, seq, grp * dk), lambda b, i: (b, 0, i)),
        out_shape=jax.ShapeDtypeStruct((batch, seq, h * dk), BF16),
        scratch_shapes=[per_head(BF16), per_head(F32), per_head(F32), per_head(F32), per_head(BF16),
                        per_head(BF16), pltpu.VMEM((grp, dk, dk), F32)],
        compiler_params=_cparams(("parallel", "parallel")), name="gdn_core",
    )(p3, p3, p3, p3, conv_w, conv_w, conv_w, beta_t, cum_t, cum_r, norm_gain.reshape(1, dk))


def _lru_kernel(x_ref, y_ref, cw_ref, cb_ref, w_ref, br_ref, bi_ref, lam_ref, o_ref, a_s, u_s):
    seq = x_ref.shape[1]
    width = x_ref.shape[2]
    x = _causal_conv(x_ref[0].astype(F32), cw_ref[...]) + cb_ref[...]
    gates = _bdot(x, w_ref[0])
    r = jax.nn.sigmoid(gates[:, :width] + br_ref[...])
    i = jax.nn.sigmoid(gates[:, width:] + bi_ref[...])
    log_a = (-LRU_C * _softplus(-lam_ref[...])) * r
    a_s[...] = jnp.exp(log_a)
    u_s[...] = jnp.sqrt(_neg_expm1(2.0 * log_a)) * (i * x)
    row = lax.broadcasted_iota(jnp.int32, (SUBLANES, width), 0)

    def scan_block(t, h_prev):
        rows = pl.ds(pl.multiple_of(t * SUBLANES, SUBLANES), SUBLANES)
        a = a_s[rows, :]
        u = u_s[rows, :]
        for d in (1, 2, 4):
            m = row >= d
            u = jnp.where(m, a * pltpu.roll(u, d, axis=0) + u, u)
            a = jnp.where(m, a * pltpu.roll(a, d, axis=0), a)
        h = u + a * h_prev
        y = jax.nn.gelu(y_ref[0, rows, :].astype(F32))
        o_ref[0, rows, :] = (h * y).astype(o_ref.dtype)
        return jnp.broadcast_to(h[SUBLANES - 1:SUBLANES, :], h.shape)

    unroll = 8

    def body(j, h_prev):
        for s in range(unroll):
            h_prev = scan_block(j * unroll + s, h_prev)
        return h_prev

    lax.fori_loop(0, seq // (SUBLANES * unroll), body, jnp.zeros((SUBLANES, width), F32))


def _lru_core(proj, conv_w, conv_b, w_r, b_r, w_i, b_i, lam, batch, seq):
    width = conv_w.shape[-1]
    nb = width // LRU_BLOCK
    p3 = proj.reshape(batch, seq, -1)
    w_ri = jnp.concatenate([w_r, w_i], axis=-1).astype(BF16)
    vec = lambda: pl.BlockSpec((1, LRU_BLOCK), lambda b, j: (0, j))
    return pl.pallas_call(
        _lru_kernel, grid=(batch, nb),
        in_specs=[pl.BlockSpec((1, seq, LRU_BLOCK), lambda b, j: (b, 0, j)),
                  pl.BlockSpec((1, seq, LRU_BLOCK), lambda b, j: (b, 0, nb + j)),
                  pl.BlockSpec((CONV_WIDTH, LRU_BLOCK), lambda b, j: (0, j)),
                  vec(),
                  pl.BlockSpec((1, LRU_BLOCK, 2 * LRU_BLOCK), lambda b, j: (j, 0, 0)),
                  vec(), vec(), vec()],
        out_specs=pl.BlockSpec((1, seq, LRU_BLOCK), lambda b, j: (b, 0, j)),
        out_shape=jax.ShapeDtypeStruct((batch, seq, width), BF16),
        scratch_shapes=[pltpu.VMEM((seq, LRU_BLOCK), F32), pltpu.VMEM((seq, LRU_BLOCK), F32)],
        compiler_params=_cparams(("parallel", "parallel")), name="lru_core",
    )(p3, p3, conv_w, conv_b.reshape(1, width), w_ri, b_r.reshape(1, width), b_i.reshape(1, width),
      lam.reshape(1, width))


def kernel(x, norm1, norm2, final_norm, ret_w_in, ret_gn_gain, ret_w_out, gdn_w_in, gdn_conv_w, gdn_a_log, gdn_dt_bias, gdn_norm_gain, gdn_w_out, gla_w_in, gla_w_gate_up, gla_gate_bias, gla_norm_gain, gla_w_out, lru_w_in, lru_conv_w, lru_conv_b, lru_w_rgate, lru_b_rgate, lru_w_igate, lru_b_igate, lru_lambda, lru_w_out, mlp_w_up, mlp_w_down):
    batch, seq, d = x.shape
    depth = norm1.shape[0]
    xf = x.reshape(batch * seq, d)
    hn = _rmsnorm(xf, norm1[0], BF16)
    for layer in range(depth):
        m, j = layer % 4, layer // 4
        if m == 0:
            w_in = ret_w_in[j]
            proj = _in_proj(hn, w_in, w_in.shape[1], BF16, "ret_in")
            o = _retention_core(proj, ret_gn_gain[j], batch, seq)
            w_out = ret_w_out[j]
        elif m == 1:
            w_in = gdn_w_in[j]
            n_main = w_in.shape[1] - 2 * GDN_HEADS
            proj = _in_proj(hn, w_in, n_main, BF16, "gdn_in")
            gates = _gate_proj(hn, w_in[:, n_main:], "gdn_gate_in")
            o = _gdn_core(proj, gates, gdn_conv_w[j], gdn_a_log[j], gdn_dt_bias[j], gdn_norm_gain[j], batch, seq)
            w_out = gdn_w_out[j]
        elif m == 2:
            w_in = gla_w_in[j]
            n_main = w_in.shape[1] - GLA_GATE_RANK
            proj = _in_proj(hn, w_in, n_main, BF16, "gla_in")
            gate_low = _gate_proj(hn, w_in[:, n_main:], "gla_gate_in")
            o = _gla_core(proj, gate_low, gla_w_gate_up[j], gla_gate_bias[j], gla_norm_gain[j], batch, seq)
            w_out = gla_w_out[j]
        else:
            w_in = lru_w_in[j]
            proj = _in_proj(hn, w_in, w_in.shape[1], BF16, "lru_in")
            o = _lru_core(proj, lru_conv_w[j], lru_conv_b[j], lru_w_rgate[j], lru_b_rgate[j].reshape(-1),
                          lru_w_igate[j], lru_b_igate[j].reshape(-1), lru_lambda[j], batch, seq)
            w_out = lru_w_out[j]
        xf, hn = _proj_res_norm(o.reshape(batch * seq, -1), w_out.astype(BF16), xf, norm2[layer])
        wu, wd = mlp_w_up[layer].astype(BF16), mlp_w_down[layer].astype(BF16)
        if layer + 1 < depth:
            xf, hn = _mlp(hn, wu, wd, xf, norm1[layer + 1], True, BF16)
        else:
            (out,) = _mlp(hn, wu, wd, xf, final_norm, False, x.dtype)
    return out.reshape(batch, seq, d)
```

```python
import functools

import jax
import jax.numpy as jnp
from jax import lax
from jax.experimental import pallas as pl
from jax.experimental.pallas import tpu as pltpu

F32 = jnp.float32
BF16 = jnp.bfloat16

CHUNK = 64
NORM_EPS = 1e-6
ROPE_BASE = 10000.0
RET_HEADS = 8
GDN_HEADS = 16
GLA_HEADS = 4
GLA_GATE_RANK = 16
GLA_TAU = 16.0
LRU_BLOCK = 128
LRU_C = 8.0
CONV_WIDTH = 4

LANES = 128
SUBLANES = 8
VMEM_LIMIT = 56 * 1024 * 1024

IN_PROJ_TILE = (1024, 1024)
OUT_PROJ_ROWS = 512
MLP_TILE = (512, 1024)
RET_GROUP = 2
GLA_GROUP = 2
CHUNK_UNROLL = 2
GDN_GROUP = 4
GDN_SOLVE_BATCH = 16


def _cparams(sem):
    return pltpu.CompilerParams(dimension_semantics=sem, vmem_limit_bytes=VMEM_LIMIT)


def _rms(x, g):
    return x * lax.rsqrt(jnp.mean(x * x, axis=-1, keepdims=True) + NORM_EPS) * g


def _softplus(x):
    return jnp.maximum(x, 0.0) + jnp.log1p(jnp.exp(-jnp.abs(x)))


def _silu(x):
    return x * jax.nn.sigmoid(x)


def _neg_expm1(z):
    u = jnp.exp(z)
    um1 = u - 1.0
    return jnp.where(um1 == 0.0, -z, -(um1 * z) / jnp.log(u))


def _bdot(a, b):
    return jnp.dot(a.astype(BF16), b.astype(BF16), preferred_element_type=F32)


def _bdot_nt(a, b):
    return lax.dot_general(a.astype(BF16), b.astype(BF16), (((1,), (1,)), ((), ())),
                           preferred_element_type=F32)


def _bdot_tn(a, b):
    return lax.dot_general(a.astype(BF16), b.astype(BF16), (((0,), (0,)), ((), ())),
                           preferred_element_type=F32)


def _bmm(a, b):
    return jnp.einsum("uij,ujk->uik", a.astype(BF16), b.astype(BF16), preferred_element_type=F32)


def _bmm_nt(a, b):
    return jnp.einsum("uid,ujd->uij", a.astype(BF16), b.astype(BF16), preferred_element_type=F32)


def _bmm_tn(a, b):
    return jnp.einsum("ucd,uce->ude", a.astype(BF16), b.astype(BF16), preferred_element_type=F32)


def _fdot(a, b):
    return jnp.dot(a, b, precision=lax.Precision.HIGHEST, preferred_element_type=F32)


def _tril_ones(n):
    r = lax.broadcasted_iota(jnp.int32, (n, n), 0)
    c = lax.broadcasted_iota(jnp.int32, (n, n), 1)
    return (r >= c).astype(F32)


def _norm_kernel(x_ref, g_ref, o_ref):
    o_ref[...] = _rms(x_ref[...], g_ref[...]).astype(o_ref.dtype)


def _rmsnorm(x, g, out_dtype, tm=512):
    m, d = x.shape
    return pl.pallas_call(
        _norm_kernel, grid=(m // tm,),
        in_specs=[pl.BlockSpec((tm, d), lambda i: (i, 0)), pl.BlockSpec((1, d), lambda i: (0, 0))],
        out_specs=pl.BlockSpec((tm, d), lambda i: (i, 0)),
        out_shape=jax.ShapeDtypeStruct((m, d), out_dtype),
        compiler_params=_cparams(("parallel",)), name="rmsnorm")(x, g.reshape(1, d))


def _in_proj_kernel(a_ref, w_ref, o_ref, wb_ref, *, w_transposed):
    @pl.when(pl.program_id(1) == 0)
    def _():
        wb_ref[...] = w_ref[...].astype(wb_ref.dtype)

    dims = (((1,), (1,)), ((), ())) if w_transposed else (((1,), (0,)), ((), ()))
    o_ref[...] = lax.dot_general(a_ref[...], wb_ref[...], dims, preferred_element_type=F32).astype(o_ref.dtype)


def _in_proj(a, w, n_cols, out_dtype, name, w_transposed=False):
    m, k = a.shape
    tm, tn = IN_PROJ_TILE
    tm, tn = min(tm, m), min(tn, n_cols)
    if w_transposed:
        w_spec = pl.BlockSpec((tn, k), lambda j, i: (j, 0))
        w_scratch = pltpu.VMEM((tn, k), BF16)
    else:
        w_spec = pl.BlockSpec((k, tn), lambda j, i: (0, j))
        w_scratch = pltpu.VMEM((k, tn), BF16)
    return pl.pallas_call(
        functools.partial(_in_proj_kernel, w_transposed=w_transposed), grid=(n_cols // tn, m // tm),
        in_specs=[pl.BlockSpec((tm, k), lambda j, i: (i, 0)), w_spec],
        out_specs=pl.BlockSpec((tm, tn), lambda j, i: (i, j)),
        out_shape=jax.ShapeDtypeStruct((m, n_cols), out_dtype),
        scratch_shapes=[w_scratch],
        compiler_params=_cparams(("parallel", "arbitrary")), name=name)(a, w)


def _gate_proj_kernel(a_ref, w_ref, o_ref):
    n = w_ref.shape[0]
    res = lax.dot_general(a_ref[...], w_ref[...].astype(BF16), (((1,), (1,)), ((), ())),
                          preferred_element_type=F32)
    o_ref[...] = jnp.zeros_like(o_ref)
    o_ref[:, :n] = res


def _gate_proj(a, w_t, row0, n_rows, name, tm=1024):
    m, k = a.shape
    tm = min(tm, m)
    return pl.pallas_call(
        _gate_proj_kernel, grid=(m // tm,),
        in_specs=[pl.BlockSpec((tm, k), lambda i: (i, 0)),
                  pl.BlockSpec((n_rows, k), lambda i: (row0 // n_rows, 0))],
        out_specs=pl.BlockSpec((tm, LANES), lambda i: (i, 0)),
        out_shape=jax.ShapeDtypeStruct((m, LANES), F32),
        compiler_params=_cparams(("parallel",)), name=name)(a, w_t)


def _proj_res_norm_kernel(a_ref, w_ref, x_ref, g_ref, xo_ref, ho_ref):
    x = x_ref[...] + jnp.dot(a_ref[...], w_ref[...], preferred_element_type=F32)
    xo_ref[...] = x
    ho_ref[...] = _rms(x, g_ref[...]).astype(ho_ref.dtype)


def _proj_res_norm(a, w, x, g):
    m, k = a.shape
    d = w.shape[1]
    tm = min(OUT_PROJ_ROWS, m)
    return pl.pallas_call(
        _proj_res_norm_kernel, grid=(m // tm,),
        in_specs=[pl.BlockSpec((tm, k), lambda i: (i, 0)),
                  pl.BlockSpec((k, d), lambda i: (0, 0), pipeline_mode=pl.Buffered(1)),
                  pl.BlockSpec((tm, d), lambda i: (i, 0)),
                  pl.BlockSpec((1, d), lambda i: (0, 0))],
        out_specs=[pl.BlockSpec((tm, d), lambda i: (i, 0)), pl.BlockSpec((tm, d), lambda i: (i, 0))],
        out_shape=[jax.ShapeDtypeStruct((m, d), F32), jax.ShapeDtypeStruct((m, d), BF16)],
        compiler_params=_cparams(("parallel",)), name="proj_res_norm")(a, w, x, g.reshape(1, d))


def _mlp_kernel(h_ref, wu_ref, wd_ref, x_ref, g_ref, *rest, write_x):
    if write_x:
        xo_ref, ho_ref, acc_ref = rest
    else:
        ho_ref, acc_ref = rest
    f = pl.program_id(1)

    @pl.when(f == 0)
    def _():
        acc_ref[...] = x_ref[...]

    u = jnp.dot(h_ref[...], wu_ref[...], preferred_element_type=F32)
    u = jnp.square(jnp.maximum(u, 0.0)).astype(BF16)
    acc_ref[...] += jnp.dot(u, wd_ref[...], preferred_element_type=F32)

    @pl.when(f == pl.num_programs(1) - 1)
    def _():
        x = acc_ref[...]
        if write_x:
            xo_ref[...] = x
        ho_ref[...] = _rms(x, g_ref[...]).astype(ho_ref.dtype)


def _mlp(h, wu, wd, layer, x, g, write_x, h_dtype):
    m, d = h.shape
    ff = wu.shape[2]
    tm, tf = MLP_TILE
    tm, tf = min(tm, m), min(tf, ff)
    row = pl.BlockSpec((tm, d), lambda i, f: (i, 0))
    out_specs = [row]
    out_shape = [jax.ShapeDtypeStruct((m, d), h_dtype)]
    if write_x:
        out_specs = [row, row]
        out_shape = [jax.ShapeDtypeStruct((m, d), F32)] + out_shape
    return pl.pallas_call(
        functools.partial(_mlp_kernel, write_x=write_x), grid=(m // tm, ff // tf),
        in_specs=[row,
                  pl.BlockSpec((None, d, tf), lambda i, f: (layer, 0, f)),
                  pl.BlockSpec((None, tf, d), lambda i, f: (layer, f, 0)),
                  row,
                  pl.BlockSpec((1, d), lambda i, f: (0, 0))],
        out_specs=out_specs, out_shape=out_shape,
        scratch_shapes=[pltpu.VMEM((tm, d), F32)],
        compiler_params=_cparams(("parallel", "arbitrary")), name="mlp")(h, wu, wd, x, g.reshape(1, d))


def _rotary(x, c, s):
    half = x.shape[-1] // 2
    x1, x2 = x[:, :half], x[:, half:]
    return jnp.concatenate([x1 * c - x2 * s, x1 * s + x2 * c], axis=-1)


def _ret_kernel(q_ref, k_ref, v_ref, g_ref, cos_ref, sin_ref, dmat_ref, qd_ref, kd_ref, cd_ref,
                gain_ref, o_ref, state_ref):
    seq = q_ref.shape[1]
    groups = state_ref.shape[0]
    dk = q_ref.shape[2] // groups
    dv = v_ref.shape[2] // groups
    state_ref[...] = jnp.zeros_like(state_ref)

    def body(n, carry):
        r = pl.ds(pl.multiple_of(n * CHUNK, CHUNK), CHUNK)
        c = cos_ref[r, :]
        s = sin_ref[r, :]
        for h in range(groups):
            kc = slice(h * dk, (h + 1) * dk)
            vc = slice(h * dv, (h + 1) * dv)
            q = _rotary(q_ref[0, r, kc].astype(F32), c, s)
            k = _rotary(k_ref[0, r, kc].astype(F32), c, s) * dk ** -0.5
            v = v_ref[0, r, vc]
            scores = _bdot_nt(q, k) * dmat_ref[h]
            state = state_ref[h]
            o = _bdot(scores, v) + _bdot(q * qd_ref[h], state)
            state_ref[h] = state * cd_ref[h] + _bdot_tn(k * kd_ref[h], v)
            o = o - jnp.mean(o, axis=-1, keepdims=True)
            o = _rms(o, gain_ref[h])
            o_ref[0, r, vc] = (o * _silu(g_ref[0, r, vc].astype(F32))).astype(o_ref.dtype)
        return carry

    lax.fori_loop(0, seq // CHUNK, body, 0, unroll=CHUNK_UNROLL)


def _retention_core(proj, gn_gain, batch, seq):
    h = RET_HEADS
    grp = RET_GROUP
    ng = h // grp
    dv = gn_gain.shape[-1]
    dk = dv // 2
    p3 = proj.reshape(batch, seq, -1)
    inv = ROPE_BASE ** (-jnp.arange(0, dk, 2, dtype=F32) / dk)
    ang = jnp.arange(seq, dtype=F32)[:, None] * inv[None, :]
    cos, sin = jnp.cos(ang), jnp.sin(ang)
    log_gamma = jnp.log1p(-jnp.exp2(-5.0 - jnp.arange(h, dtype=F32)))
    pos = jnp.arange(CHUNK, dtype=F32)
    dist = jnp.abs(pos[:, None] - pos[None, :])
    dmat = jnp.exp(log_gamma[:, None, None] * dist)
    qd = jnp.exp(log_gamma[:, None] * (pos + 1.0))[:, :, None]
    kd = jnp.exp(log_gamma[:, None] * (CHUNK - 1.0 - pos))[:, :, None]
    cd = jnp.exp(log_gamma * CHUNK)[:, None, None]
    per_head = lambda shape: pl.BlockSpec((grp,) + shape, lambda b, i: (i, 0, 0))
    return pl.pallas_call(
        _ret_kernel, grid=(batch, ng),
        in_specs=[pl.BlockSpec((1, seq, grp * dk), lambda b, i: (b, 0, i)),
                  pl.BlockSpec((1, seq, grp * dk), lambda b, i: (b, 0, ng + i)),
                  pl.BlockSpec((1, seq, grp * dv), lambda b, i: (b, 0, ng + i)),
                  pl.BlockSpec((1, seq, grp * dv), lambda b, i: (b, 0, 2 * ng + i)),
                  pl.BlockSpec((seq, dk // 2), lambda b, i: (0, 0)),
                  pl.BlockSpec((seq, dk // 2), lambda b, i: (0, 0)),
                  per_head((CHUNK, CHUNK)), per_head((CHUNK, 1)), per_head((CHUNK, 1)),
                  per_head((1, 1)), per_head((1, dv))],
        out_specs=pl.BlockSpec((1, seq, grp * dv), lambda b, i: (b, 0, i)),
        out_shape=jax.ShapeDtypeStruct((batch, seq, h * dv), BF16),
        scratch_shapes=[pltpu.VMEM((grp, dk, dv), F32)],
        compiler_params=_cparams(("parallel", "parallel")), name="retention_core",
    )(p3, p3, p3, p3, cos, sin, dmat, qd, kd, cd, gn_gain.reshape(h, 1, dv))


def _gla_kernel(q_ref, k_ref, v_ref, r_ref, gl_ref, wg_ref, gb_ref, gain_ref, o_ref, state_ref):
    seq = q_ref.shape[1]
    groups = state_ref.shape[0]
    dk = q_ref.shape[2] // groups
    dv = v_ref.shape[2] // groups
    state_ref[...] = jnp.zeros_like(state_ref)
    tri = _tril_ones(CHUNK)
    row = lax.broadcasted_iota(jnp.int32, (CHUNK, CHUNK), 0)
    col = lax.broadcasted_iota(jnp.int32, (CHUNK, CHUNK), 1)

    def body(n, carry):
        r = pl.ds(pl.multiple_of(n * CHUNK, CHUNK), CHUNK)
        logit = _fdot(gl_ref[0, r, :], wg_ref[...]) + gb_ref[...]
        log_alpha = -_softplus(-logit) / GLA_TAU
        cum_all = _fdot(tri, log_alpha)
        for h in range(groups):
            kc = slice(h * dk, (h + 1) * dk)
            vc = slice(h * dv, (h + 1) * dv)
            cum = cum_all[:, kc]
            ref = cum[CHUNK // 2 - 1:CHUNK // 2, :]
            last = cum[CHUNK - 1:CHUNK, :]
            fwd, bwd = jnp.exp(cum - ref), jnp.exp(ref - cum)
            q = q_ref[0, r, kc].astype(F32) * dk ** -0.5
            k = k_ref[0, r, kc].astype(F32)
            v = v_ref[0, r, vc]
            s_lo = _bdot_nt(q * fwd, k * bwd)
            s_up = _bdot_nt(q * bwd, k * fwd)
            scores = jnp.where(row >= col, s_lo, s_up)
            state = state_ref[h]
            o = _bdot(scores, v) + _bdot_nt(q * jnp.exp(cum), state)
            state_ref[h] = state * jnp.exp(last) + _bdot_tn(v, k * jnp.exp(last - cum))
            o = _rms(o, gain_ref[h])
            o_ref[0, r, vc] = (o * _silu(r_ref[0, r, vc].astype(F32))).astype(o_ref.dtype)
        return carry

    lax.fori_loop(0, seq // CHUNK, body, 0, unroll=CHUNK_UNROLL)


def _gla_core(proj, gate_low, w_gate_up, gate_bias, norm_gain, batch, seq):
    h = GLA_HEADS
    grp = GLA_GROUP
    ng = h // grp
    dv = norm_gain.shape[-1]
    dk = dv // 2
    p3 = proj.reshape(batch, seq, -1)
    gl3 = gate_low.reshape(batch, seq, LANES)
    wg = jnp.zeros((LANES, h * dk), F32).at[:GLA_GATE_RANK].set(w_gate_up)
    return pl.pallas_call(
        _gla_kernel, grid=(batch, ng),
        in_specs=[pl.BlockSpec((1, seq, grp * dk), lambda b, i: (b, 0, i)),
                  pl.BlockSpec((1, seq, grp * dk), lambda b, i: (b, 0, ng + i)),
                  pl.BlockSpec((1, seq, grp * dv), lambda b, i: (b, 0, ng + i)),
                  pl.BlockSpec((1, seq, grp * dv), lambda b, i: (b, 0, 2 * ng + i)),
                  pl.BlockSpec((1, seq, LANES), lambda b, i: (b, 0, 0)),
                  pl.BlockSpec((LANES, grp * dk), lambda b, i: (0, i)),
                  pl.BlockSpec((1, grp * dk), lambda b, i: (0, i)),
                  pl.BlockSpec((grp, 1, dv), lambda b, i: (i, 0, 0))],
        out_specs=pl.BlockSpec((1, seq, grp * dv), lambda b, i: (b, 0, i)),
        out_shape=jax.ShapeDtypeStruct((batch, seq, h * dv), BF16),
        scratch_shapes=[pltpu.VMEM((grp, dv, dk), F32)],
        compiler_params=_cparams(("parallel", "parallel")), name="gla_core",
    )(p3, p3, p3, p3, gl3, wg, gate_bias.reshape(1, h * dk), norm_gain.reshape(h, 1, dv))


def _causal_conv(x, w):
    width = w.shape[0]
    xp = jnp.concatenate([jnp.zeros((SUBLANES, x.shape[1]), x.dtype), x], axis=0)
    out = x * w[width - 1:width, :]
    for sh in range(1, width):
        out = out + pltpu.roll(xp, sh, axis=0)[SUBLANES:] * w[width - 1 - sh:width - sh, :]
    return out


def _l2norm(x):
    return x * lax.rsqrt(jnp.sum(x * x, axis=-1, keepdims=True) + NORM_EPS)


def _gdn_gate_kernel(bl_ref, al_ref, alog_ref, dt_ref, beta_ref, cum_ref):
    beta_ref[0] = jax.nn.sigmoid(bl_ref[0])
    log_alpha = -jnp.exp(alog_ref[...]) * _softplus(al_ref[0] + dt_ref[...])
    cum_ref[0] = _fdot(_tril_ones(CHUNK), log_alpha)


def _gdn_gates(gates, a_log, dt_bias, batch, seq):
    h = GDN_HEADS
    n = seq // CHUNK
    g4 = gates.reshape(batch, n, CHUNK, LANES)

    def to_pos_major(x):
        return x.transpose(0, 2, 3, 1).reshape(batch, CHUNK, h * n)

    spec = pl.BlockSpec((1, CHUNK, h * n), lambda b: (b, 0, 0))
    tab = pl.BlockSpec((1, h * n), lambda b: (0, 0))
    return pl.pallas_call(
        _gdn_gate_kernel, grid=(batch,),
        in_specs=[spec, spec, tab, tab], out_specs=[spec, spec],
        out_shape=[jax.ShapeDtypeStruct((batch, CHUNK, h * n), F32)] * 2,
        compiler_params=_cparams(("parallel",)), name="gdn_gates",
    )(to_pos_major(g4[..., :h]), to_pos_major(g4[..., h:2 * h]),
      jnp.repeat(a_log.astype(F32), n).reshape(1, h * n),
      jnp.repeat(dt_bias.astype(F32), n).reshape(1, h * n))


def _unit_lower_inverse_minus_eye(a, row, col):
    def blk(size):
        return (row // size) == (col // size)

    d = jnp.where(blk(8), a, 0.0)
    d2 = _bmm(d, d)
    d3 = _bmm(d2, d)
    d4 = _bmm(d2, d2)
    q = d2 - d - d3
    y = q + d4 + _bmm(q, d4)
    for size in (16, 32, 64):
        e = jnp.where(blk(size) & jnp.logical_not(blk(size // 2)), a, 0.0)
        t = e + _bmm(y, e)
        y = y - t - _bmm(t, y)
    return y


def _lane_column(x, lane_ids, idx):
    return jnp.sum(jnp.where(lane_ids == idx, x, 0.0), axis=1, keepdims=True)


def _gdn_kernel(q_ref, k_ref, v_ref, z_ref, wq_ref, wk_ref, wv_ref, betat_ref, cumt_ref, cumr_ref,
                gain_ref, o_ref, q_s, k_s, v_s, m_s, g_s, sb_s, st_s):
    seq = q_ref.shape[1]
    groups = st_s.shape[0]
    dk = st_s.shape[1]
    n_chunks = seq // CHUNK
    ub = min(GDN_SOLVE_BATCH, n_chunks)
    row = lax.broadcasted_iota(jnp.int32, (CHUNK, CHUNK), 0)
    col = lax.broadcasted_iota(jnp.int32, (CHUNK, CHUNK), 1)
    strict = row > col
    betat = betat_ref[0]
    cumt = cumt_ref[0]
    lane_ids = lax.broadcasted_iota(jnp.int32, cumt.shape, 1)
    dec_all = jnp.exp(cumt[CHUNK - 1:CHUNK, :])
    gain = gain_ref[...]

    for g in range(groups):
        cs = slice(g * dk, (g + 1) * dk)
        qg = _l2norm(_silu(_causal_conv(q_ref[0, :, cs].astype(F32), wq_ref[:, cs])))
        q_s[g] = (qg * dk ** -0.5).astype(q_s.dtype)
        k_s[g] = _l2norm(_silu(_causal_conv(k_ref[0, :, cs].astype(F32), wk_ref[:, cs])))
        v_s[g] = _silu(_causal_conv(v_ref[0, :, cs].astype(F32), wv_ref[:, cs]))

    def solve(g, n0):
        r = pl.ds(pl.multiple_of(n0 * CHUNK, ub * CHUNK), ub * CHUNK)
        k = k_s[g, r, :].reshape(ub, CHUNK, dk)
        v = v_s[g, r, :].reshape(ub, CHUNK, dk)
        cum_c = jnp.stack([_lane_column(cumt, lane_ids, g * n_chunks + n0 + u) for u in range(ub)])
        beta = jnp.stack([_lane_column(betat, lane_ids, g * n_chunks + n0 + u) for u in range(ub)])
        cum_r = cumr_ref[0, g, pl.ds(n0, ub)]
        rel = jnp.where(strict, jnp.exp(jnp.where(strict, cum_c - cum_r, 0.0)), 0.0)
        a = beta * rel * _bmm_nt(k, k)
        y = _unit_lower_inverse_minus_eye(a, row, col)
        rhs = jnp.concatenate([beta * v, (beta * jnp.exp(cum_c)) * k], axis=-1)
        sol = rhs + _bmm(y, rhs)
        ke = k * jnp.exp(cum_c[:, CHUNK - 1:CHUNK, :] - cum_c)
        gm = _bmm_tn(ke, sol)
        rs = pl.ds(pl.multiple_of(n0 * dk, ub * dk), ub * dk)
        g_s[g, rs, :] = gm[..., :dk].reshape(ub * dk, dk)
        m_s[g, rs, :] = gm[..., dk:].reshape(ub * dk, dk).astype(m_s.dtype)

    def solve_body(i, carry):
        for g in range(groups):
            solve(g, i * ub)
        return carry

    lax.fori_loop(0, n_chunks // ub, solve_body, 0)

    st_s[...] = jnp.zeros_like(st_s)

    def state_body(n, carry):
        rs = pl.ds(pl.multiple_of(n * dk, dk), dk)
        for g in range(groups):
            dec = _lane_column(dec_all, lane_ids[:1], g * n_chunks + n)
            state = st_s[g]
            state = state * dec - _bdot(m_s[g, rs, :], state) + g_s[g, rs, :]
            st_s[g] = state
            sb_s[g, rs, :] = state.astype(sb_s.dtype)
        return carry

    lax.fori_loop(0, n_chunks, state_body, 0)

    def readout(g, n0):
        r = pl.ds(pl.multiple_of(n0 * CHUNK, ub * CHUNK), ub * CHUNK)
        rs = pl.ds(pl.multiple_of(n0 * dk, ub * dk), ub * dk)
        cs = slice(g * dk, (g + 1) * dk)
        o = _bmm(q_s[g, r, :].reshape(ub, CHUNK, dk), sb_s[g, rs, :].reshape(ub, dk, dk))
        o = _rms(o.reshape(ub * CHUNK, dk), gain)
        o_ref[0, r, cs] = (o * _silu(z_ref[0, r, cs].astype(F32))).astype(o_ref.dtype)

    def readout_body(i, carry):
        for g in range(groups):
            readout(g, i * ub)
        return carry

    lax.fori_loop(0, n_chunks // ub, readout_body, 0)


def _gdn_core(proj, gates, conv_w, a_log, dt_bias, norm_gain, batch, seq):
    h = GDN_HEADS
    grp = GDN_GROUP
    ng = h // grp
    dk = norm_gain.shape[-1]
    n = seq // CHUNK
    p3 = proj.reshape(batch, seq, -1)
    beta_t, cum_t = _gdn_gates(gates, a_log, dt_bias, batch, seq)
    cum_r = cum_t.reshape(batch, CHUNK, h, n).transpose(0, 2, 3, 1)[:, :, :, None, :]
    head = lambda off: pl.BlockSpec((1, seq, grp * dk), lambda b, i: (b, 0, off + i))
    convw = lambda off: pl.BlockSpec((CONV_WIDTH, grp * dk), lambda b, i: (0, off + i))
    gate_spec = pl.BlockSpec((1, CHUNK, grp * n), lambda b, i: (b, 0, i))
    per_head = lambda dt: pltpu.VMEM((grp, seq, dk), dt)
    per_chunk = lambda dt: pltpu.VMEM((grp, n * dk, dk), dt)
    return pl.pallas_call(
        _gdn_kernel, grid=(batch, ng),
        in_specs=[head(0), head(ng), head(2 * ng), head(3 * ng),
                  convw(0), convw(ng), convw(2 * ng),
                  gate_spec, gate_spec,
                  pl.BlockSpec((1, grp, n, 1, CHUNK), lambda b, i: (b, i, 0, 0, 0)),
                  pl.BlockSpec((1, dk), lambda b, i: (0, 0))],
        out_specs=pl.BlockSpec((1, seq, grp * dk), lambda b, i: (b, 0, i)),
        out_shape=jax.ShapeDtypeStruct((batch, seq, h * dk), BF16),
        scratch_shapes=[per_head(BF16), per_head(F32), per_head(F32), per_chunk(BF16), per_chunk(F32),
                        per_chunk(BF16), pltpu.VMEM((grp, dk, dk), F32)],
        compiler_params=_cparams(("parallel", "parallel")), name="gdn_core",
    )(p3, p3, p3, p3, conv_w, conv_w, conv_w, beta_t, cum_t, cum_r, norm_gain.reshape(1, dk))


def _lru_kernel(x_ref, y_ref, cw_ref, cb_ref, w_ref, br_ref, bi_ref, lam_ref, o_ref, a_s, u_s):
    seq = x_ref.shape[1]
    width = x_ref.shape[2]
    x = _causal_conv(x_ref[0].astype(F32), cw_ref[...]) + cb_ref[...]
    gates = _bdot(x, w_ref[0])
    r = jax.nn.sigmoid(gates[:, :width] + br_ref[...])
    i = jax.nn.sigmoid(gates[:, width:] + bi_ref[...])
    log_a = (-LRU_C * _softplus(-lam_ref[...])) * r
    a_s[...] = jnp.exp(log_a)
    u_s[...] = jnp.sqrt(_neg_expm1(2.0 * log_a)) * (i * x)
    row = lax.broadcasted_iota(jnp.int32, (SUBLANES, width), 0)

    def scan_block(t, h_prev):
        rows = pl.ds(pl.multiple_of(t * SUBLANES, SUBLANES), SUBLANES)
        a = a_s[rows, :]
        u = u_s[rows, :]
        for d in (1, 2, 4):
            m = row >= d
            u = jnp.where(m, a * pltpu.roll(u, d, axis=0) + u, u)
            a = jnp.where(m, a * pltpu.roll(a, d, axis=0), a)
        h = u + a * h_prev
        y = jax.nn.gelu(y_ref[0, rows, :].astype(F32))
        o_ref[0, rows, :] = (h * y).astype(o_ref.dtype)
        return jnp.broadcast_to(h[SUBLANES - 1:SUBLANES, :], h.shape)

    unroll = 8

    def body(j, h_prev):
        for s in range(unroll):
            h_prev = scan_block(j * unroll + s, h_prev)
        return h_prev

    lax.fori_loop(0, seq // (SUBLANES * unroll), body, jnp.zeros((SUBLANES, width), F32))


def _lru_core(proj, conv_w, conv_b, w_r, b_r, w_i, b_i, lam, batch, seq):
    width = conv_w.shape[-1]
    nb = width // LRU_BLOCK
    p3 = proj.reshape(batch, seq, -1)
    w_ri = jnp.concatenate([w_r, w_i], axis=-1).astype(BF16)
    vec = lambda: pl.BlockSpec((1, LRU_BLOCK), lambda b, j: (0, j))
    return pl.pallas_call(
        _lru_kernel, grid=(batch, nb),
        in_specs=[pl.BlockSpec((1, seq, LRU_BLOCK), lambda b, j: (b, 0, j)),
                  pl.BlockSpec((1, seq, LRU_BLOCK), lambda b, j: (b, 0, nb + j)),
                  pl.BlockSpec((CONV_WIDTH, LRU_BLOCK), lambda b, j: (0, j)),
                  vec(),
                  pl.BlockSpec((1, LRU_BLOCK, 2 * LRU_BLOCK), lambda b, j: (j, 0, 0)),
                  vec(), vec(), vec()],
        out_specs=pl.BlockSpec((1, seq, LRU_BLOCK), lambda b, j: (b, 0, j)),
        out_shape=jax.ShapeDtypeStruct((batch, seq, width), BF16),
        scratch_shapes=[pltpu.VMEM((seq, LRU_BLOCK), F32), pltpu.VMEM((seq, LRU_BLOCK), F32)],
        compiler_params=_cparams(("parallel", "parallel")), name="lru_core",
    )(p3, p3, conv_w, conv_b.reshape(1, width), w_ri, b_r.reshape(1, width), b_i.reshape(1, width),
      lam.reshape(1, width))


def kernel(x, norm1, norm2, final_norm, ret_w_in, ret_gn_gain, ret_w_out, gdn_w_in, gdn_conv_w, gdn_a_log, gdn_dt_bias, gdn_norm_gain, gdn_w_out, gla_w_in, gla_w_gate_up, gla_gate_bias, gla_norm_gain, gla_w_out, lru_w_in, lru_conv_w, lru_conv_b, lru_w_rgate, lru_b_rgate, lru_w_igate, lru_b_igate, lru_lambda, lru_w_out, mlp_w_up, mlp_w_down):
    batch, seq, d = x.shape
    depth = norm1.shape[0]
    xf = x.reshape(batch * seq, d)
    hn = _rmsnorm(xf, norm1[0], BF16)
    wu, wd = mlp_w_up.astype(BF16), mlp_w_down.astype(BF16)
    for layer in range(depth):
        m, j = layer % 4, layer // 4
        if m == 0:
            w_in = ret_w_in[j]
            proj = _in_proj(hn, w_in, w_in.shape[1], BF16, "ret_in")
            o = _retention_core(proj, ret_gn_gain[j], batch, seq)
            w_out = ret_w_out[j]
        elif m == 1:
            w_t = gdn_w_in[j].T
            n_main = w_t.shape[0] - 2 * GDN_HEADS
            proj = _in_proj(hn, w_t, n_main, BF16, "gdn_in", w_transposed=True)
            gates = _gate_proj(hn, w_t, n_main, 2 * GDN_HEADS, "gdn_gate_in")
            o = _gdn_core(proj, gates, gdn_conv_w[j], gdn_a_log[j], gdn_dt_bias[j], gdn_norm_gain[j], batch, seq)
            w_out = gdn_w_out[j]
        elif m == 2:
            w_t = gla_w_in[j].T
            n_main = w_t.shape[0] - GLA_GATE_RANK
            proj = _in_proj(hn, w_t, n_main, BF16, "gla_in", w_transposed=True)
            gate_low = _gate_proj(hn, w_t, n_main, GLA_GATE_RANK, "gla_gate_in")
            o = _gla_core(proj, gate_low, gla_w_gate_up[j], gla_gate_bias[j], gla_norm_gain[j], batch, seq)
            w_out = gla_w_out[j]
        else:
            w_in = lru_w_in[j]
            proj = _in_proj(hn, w_in, w_in.shape[1], BF16, "lru_in")
            o = _lru_core(proj, lru_conv_w[j], lru_conv_b[j], lru_w_rgate[j], lru_b_rgate[j].reshape(-1),
                          lru_w_igate[j], lru_b_igate[j].reshape(-1), lru_lambda[j], batch, seq)
            w_out = lru_w_out[j]
        xf, hn = _proj_res_norm(o.reshape(batch * seq, -1), w_out.astype(BF16), xf, norm2[layer])
        if layer + 1 < depth:
            xf, hn = _mlp(hn, wu, wd, layer, xf, norm1[layer + 1], True, BF16)
        else:
            (out,) = _mlp(hn, wu, wd, layer, xf, final_norm, False, x.dtype)
    return out.reshape(batch, seq, d)
```

```python
import functools

import jax
import jax.numpy as jnp
from jax import lax
from jax.experimental import pallas as pl
from jax.experimental.pallas import tpu as pltpu

F32 = jnp.float32
BF16 = jnp.bfloat16

CHUNK = 64
NORM_EPS = 1e-6
ROPE_BASE = 10000.0
RET_HEADS = 8
GDN_HEADS = 16
GLA_HEADS = 4
GLA_GATE_RANK = 16
GLA_TAU = 16.0
LRU_BLOCK = 128
LRU_C = 8.0
CONV_WIDTH = 4

LANES = 128
SUBLANES = 8
VMEM_LIMIT = 56 * 1024 * 1024

IN_PROJ_TILE = (1024, 1024)
OUT_PROJ_ROWS = 512
MLP_TILE = (512, 1024)
RET_GROUP = 4
GLA_GROUP = 4
CHUNK_UNROLL = 2
SEQ_BLOCK = 512
GDN_GROUP = 4
GDN_SOLVE_BATCH = 16


def _cparams(sem):
    return pltpu.CompilerParams(dimension_semantics=sem, vmem_limit_bytes=VMEM_LIMIT)


def _rms(x, g):
    return x * lax.rsqrt(jnp.mean(x * x, axis=-1, keepdims=True) + NORM_EPS) * g


def _softplus(x):
    return jnp.maximum(x, 0.0) + jnp.log1p(jnp.exp(-jnp.abs(x)))


def _silu(x):
    return x * jax.nn.sigmoid(x)


def _bdot(a, b):
    return jnp.dot(a.astype(BF16), b.astype(BF16), preferred_element_type=F32)


def _bdot_nt(a, b):
    return lax.dot_general(a.astype(BF16), b.astype(BF16), (((1,), (1,)), ((), ())),
                           preferred_element_type=F32)


def _bdot_tn(a, b):
    return lax.dot_general(a.astype(BF16), b.astype(BF16), (((0,), (0,)), ((), ())),
                           preferred_element_type=F32)


def _bmm(a, b):
    return jnp.einsum("uij,ujk->uik", a.astype(BF16), b.astype(BF16), preferred_element_type=F32)


def _bmm_nt(a, b):
    return jnp.einsum("uid,ujd->uij", a.astype(BF16), b.astype(BF16), preferred_element_type=F32)


def _bmm_tn(a, b):
    return jnp.einsum("ucd,uce->ude", a.astype(BF16), b.astype(BF16), preferred_element_type=F32)


def _fdot(a, b):
    return jnp.dot(a, b, precision=lax.Precision.HIGHEST, preferred_element_type=F32)


def _tril_ones(n):
    r = lax.broadcasted_iota(jnp.int32, (n, n), 0)
    c = lax.broadcasted_iota(jnp.int32, (n, n), 1)
    return (r >= c).astype(F32)


def _norm_kernel(x_ref, g_ref, o_ref):
    o_ref[...] = _rms(x_ref[...], g_ref[...]).astype(o_ref.dtype)


def _rmsnorm(x, g, out_dtype, tm=512):
    m, d = x.shape
    return pl.pallas_call(
        _norm_kernel, grid=(m // tm,),
        in_specs=[pl.BlockSpec((tm, d), lambda i: (i, 0)), pl.BlockSpec((1, d), lambda i: (0, 0))],
        out_specs=pl.BlockSpec((tm, d), lambda i: (i, 0)),
        out_shape=jax.ShapeDtypeStruct((m, d), out_dtype),
        compiler_params=_cparams(("parallel",)), name="rmsnorm")(x, g.reshape(1, d))


def _in_proj_kernel(a_ref, w_ref, o_ref, wb_ref, *, w_transposed):
    @pl.when(pl.program_id(1) == 0)
    def _():
        wb_ref[...] = w_ref[...].astype(wb_ref.dtype)

    dims = (((1,), (1,)), ((), ())) if w_transposed else (((1,), (0,)), ((), ()))
    o_ref[...] = lax.dot_general(a_ref[...], wb_ref[...], dims, preferred_element_type=F32).astype(o_ref.dtype)


def _in_proj(a, w, n_cols, out_dtype, name, w_transposed=False):
    m, k = a.shape
    tm, tn = IN_PROJ_TILE
    tm, tn = min(tm, m), min(tn, n_cols)
    if w_transposed:
        w_spec = pl.BlockSpec((tn, k), lambda j, i: (j, 0))
        w_scratch = pltpu.VMEM((tn, k), BF16)
    else:
        w_spec = pl.BlockSpec((k, tn), lambda j, i: (0, j))
        w_scratch = pltpu.VMEM((k, tn), BF16)
    return pl.pallas_call(
        functools.partial(_in_proj_kernel, w_transposed=w_transposed), grid=(n_cols // tn, m // tm),
        in_specs=[pl.BlockSpec((tm, k), lambda j, i: (i, 0)), w_spec],
        out_specs=pl.BlockSpec((tm, tn), lambda j, i: (i, j)),
        out_shape=jax.ShapeDtypeStruct((m, n_cols), out_dtype),
        scratch_shapes=[w_scratch],
        compiler_params=_cparams(("parallel", "arbitrary")), name=name)(a, w)


def _gate_proj_kernel(a_ref, w_ref, o_ref):
    n = w_ref.shape[0]
    res = lax.dot_general(a_ref[...], w_ref[...].astype(BF16), (((1,), (1,)), ((), ())),
                          preferred_element_type=F32)
    o_ref[...] = jnp.zeros_like(o_ref)
    o_ref[:, :n] = res


def _gate_proj(a, w_t, row0, n_rows, name, tm=1024):
    m, k = a.shape
    tm = min(tm, m)
    return pl.pallas_call(
        _gate_proj_kernel, grid=(m // tm,),
        in_specs=[pl.BlockSpec((tm, k), lambda i: (i, 0)),
                  pl.BlockSpec((n_rows, k), lambda i: (row0 // n_rows, 0))],
        out_specs=pl.BlockSpec((tm, LANES), lambda i: (i, 0)),
        out_shape=jax.ShapeDtypeStruct((m, LANES), F32),
        compiler_params=_cparams(("parallel",)), name=name)(a, w_t)


def _proj_res_norm_kernel(a_ref, w_ref, x_ref, g_ref, xo_ref, ho_ref):
    x = x_ref[...] + jnp.dot(a_ref[...], w_ref[...], preferred_element_type=F32)
    xo_ref[...] = x
    ho_ref[...] = _rms(x, g_ref[...]).astype(ho_ref.dtype)


def _proj_res_norm(a, w, x, g):
    m, k = a.shape
    d = w.shape[1]
    tm = min(OUT_PROJ_ROWS, m)
    return pl.pallas_call(
        _proj_res_norm_kernel, grid=(m // tm,),
        in_specs=[pl.BlockSpec((tm, k), lambda i: (i, 0)),
                  pl.BlockSpec((k, d), lambda i: (0, 0), pipeline_mode=pl.Buffered(1)),
                  pl.BlockSpec((tm, d), lambda i: (i, 0)),
                  pl.BlockSpec((1, d), lambda i: (0, 0))],
        out_specs=[pl.BlockSpec((tm, d), lambda i: (i, 0)), pl.BlockSpec((tm, d), lambda i: (i, 0))],
        out_shape=[jax.ShapeDtypeStruct((m, d), F32), jax.ShapeDtypeStruct((m, d), BF16)],
        compiler_params=_cparams(("parallel",)), name="proj_res_norm")(a, w, x, g.reshape(1, d))


def _mlp_kernel(*refs, write_x, convert):
    h_ref, wu_ref, wd_ref, x_ref, g_ref = refs[:5]
    refs = refs[5:]
    f32_refs, refs = refs[:convert], refs[convert:]
    if write_x:
        xo_ref = refs[0]
        refs = refs[1:]
    ho_ref = refs[0]
    refs = refs[1:]
    bf16_refs, refs = refs[:convert], refs[convert:]
    acc_ref, u_ref = refs
    s = pl.program_id(1)
    last = pl.num_programs(1) - 1
    slot = s % 2

    def up():
        u = jnp.dot(h_ref[...], wu_ref[...], preferred_element_type=F32)
        return jnp.square(jnp.maximum(u, 0.0)).astype(u_ref.dtype)

    def down():
        return jnp.dot(u_ref[1 - slot], wd_ref[...], preferred_element_type=F32)

    @pl.when(s == 0)
    def _():
        acc_ref[...] = x_ref[...]
        u_ref[slot] = up()

    @pl.when(jnp.logical_and(s > 0, s < last))
    def _():
        acc_ref[...] += down()
        u_ref[slot] = up()

    @pl.when(s == last)
    def _():
        x = acc_ref[...] + down()
        if write_x:
            xo_ref[...] = x
        ho_ref[...] = _rms(x, g_ref[...]).astype(ho_ref.dtype)

    for src, dst in zip(f32_refs, bf16_refs):
        dst[...] = src[...].astype(dst.dtype)


def _mlp(h, wu, wd, x, g, write_x, h_dtype, next_w=()):
    m, d = h.shape
    ff = wu.shape[1]
    tm, tf = MLP_TILE
    tm, tf = min(tm, m), min(tf, ff)
    n_ff = ff // tf
    row = pl.BlockSpec((tm, d), lambda i, s: (i, 0))
    in_specs = [row,
                pl.BlockSpec((d, tf), lambda i, s: (0, jnp.minimum(s, n_ff - 1))),
                pl.BlockSpec((tf, d), lambda i, s: (jnp.maximum(s - 1, 0), 0)),
                row,
                pl.BlockSpec((1, d), lambda i, s: (0, 0))]
    args = [h, wu, wd, x, g.reshape(1, d)]
    out_specs = [row]
    out_shape = [jax.ShapeDtypeStruct((m, d), h_dtype)]
    if write_x:
        out_specs = [row, row]
        out_shape = [jax.ShapeDtypeStruct((m, d), F32)] + out_shape
    n_slabs = (m // tm) * n_ff
    slab = lambda i, s: i * n_ff + jnp.minimum(s, n_ff - 1)
    for w_all, layer in next_w:
        _, rows, cols = w_all.shape
        in_specs.append(pl.BlockSpec((None, rows // n_slabs, cols), lambda i, s, l=layer: (l, slab(i, s), 0)))
        args.append(w_all)
        out_specs.append(pl.BlockSpec((rows // n_slabs, cols), lambda i, s: (slab(i, s), 0)))
        out_shape.append(jax.ShapeDtypeStruct((rows, cols), BF16))
    return pl.pallas_call(
        functools.partial(_mlp_kernel, write_x=write_x, convert=len(next_w)),
        grid=(m // tm, n_ff + 1),
        in_specs=in_specs, out_specs=out_specs, out_shape=out_shape,
        scratch_shapes=[pltpu.VMEM((tm, d), F32), pltpu.VMEM((2, tm, tf), BF16)],
        compiler_params=_cparams(("parallel", "arbitrary")), name="mlp")(*args)


def _rotary(x, c, s):
    half = x.shape[-1] // 2
    x1, x2 = x[:, :half], x[:, half:]
    return jnp.concatenate([x1 * c - x2 * s, x1 * s + x2 * c], axis=-1)


def _ret_kernel(q_ref, k_ref, v_ref, g_ref, cos_ref, sin_ref, dmat_ref, qd_ref, kd_ref, cd_ref,
                gain_ref, o_ref, state_ref):
    seq = q_ref.shape[1]
    groups = state_ref.shape[0]
    dk = q_ref.shape[2] // groups
    dv = v_ref.shape[2] // groups

    @pl.when(pl.program_id(2) == 0)
    def _():
        state_ref[...] = jnp.zeros_like(state_ref)

    def body(n, carry):
        r = pl.ds(pl.multiple_of(n * CHUNK, CHUNK), CHUNK)
        c = cos_ref[r, :]
        s = sin_ref[r, :]
        for h in range(groups):
            kc = slice(h * dk, (h + 1) * dk)
            vc = slice(h * dv, (h + 1) * dv)
            q = _rotary(q_ref[0, r, kc].astype(F32), c, s)
            k = _rotary(k_ref[0, r, kc].astype(F32), c, s) * dk ** -0.5
            v = v_ref[0, r, vc]
            scores = _bdot_nt(q, k) * dmat_ref[h]
            state = state_ref[h]
            o = _bdot(scores, v) + _bdot(q * qd_ref[h], state)
            state_ref[h] = state * cd_ref[h] + _bdot_tn(k * kd_ref[h], v)
            o = o - jnp.mean(o, axis=-1, keepdims=True)
            o = _rms(o, gain_ref[h])
            o_ref[0, r, vc] = (o * _silu(g_ref[0, r, vc].astype(F32))).astype(o_ref.dtype)
        return carry

    lax.fori_loop(0, seq // CHUNK, body, 0, unroll=CHUNK_UNROLL)


def _retention_core(proj, gn_gain, batch, seq):
    h = RET_HEADS
    grp = RET_GROUP
    ng = h // grp
    dv = gn_gain.shape[-1]
    dk = dv // 2
    p3 = proj.reshape(batch, seq, -1)
    inv = ROPE_BASE ** (-jnp.arange(0, dk, 2, dtype=F32) / dk)
    ang = jnp.arange(seq, dtype=F32)[:, None] * inv[None, :]
    cos, sin = jnp.cos(ang), jnp.sin(ang)
    log_gamma = jnp.log1p(-jnp.exp2(-5.0 - jnp.arange(h, dtype=F32)))
    pos = jnp.arange(CHUNK, dtype=F32)
    dist = jnp.abs(pos[:, None] - pos[None, :])
    dmat = jnp.exp(log_gamma[:, None, None] * dist)
    qd = jnp.exp(log_gamma[:, None] * (pos + 1.0))[:, :, None]
    kd = jnp.exp(log_gamma[:, None] * (CHUNK - 1.0 - pos))[:, :, None]
    cd = jnp.exp(log_gamma * CHUNK)[:, None, None]
    sb = min(SEQ_BLOCK, seq)
    per_head = lambda shape: pl.BlockSpec((grp,) + shape, lambda b, i, t: (i, 0, 0))
    return pl.pallas_call(
        _ret_kernel, grid=(batch, ng, seq // sb),
        in_specs=[pl.BlockSpec((1, sb, grp * dk), lambda b, i, t: (b, t, i)),
                  pl.BlockSpec((1, sb, grp * dk), lambda b, i, t: (b, t, ng + i)),
                  pl.BlockSpec((1, sb, grp * dv), lambda b, i, t: (b, t, ng + i)),
                  pl.BlockSpec((1, sb, grp * dv), lambda b, i, t: (b, t, 2 * ng + i)),
                  pl.BlockSpec((sb, dk // 2), lambda b, i, t: (t, 0)),
                  pl.BlockSpec((sb, dk // 2), lambda b, i, t: (t, 0)),
                  per_head((CHUNK, CHUNK)), per_head((CHUNK, 1)), per_head((CHUNK, 1)),
                  per_head((1, 1)), per_head((1, dv))],
        out_specs=pl.BlockSpec((1, sb, grp * dv), lambda b, i, t: (b, t, i)),
        out_shape=jax.ShapeDtypeStruct((batch, seq, h * dv), BF16),
        scratch_shapes=[pltpu.VMEM((grp, dk, dv), F32)],
        compiler_params=_cparams(("parallel", "parallel", "arbitrary")), name="retention_core",
    )(p3, p3, p3, p3, cos, sin, dmat, qd, kd, cd, gn_gain.reshape(h, 1, dv))


def _gla_kernel(q_ref, k_ref, v_ref, r_ref, gl_ref, wg_ref, gb_ref, gain_ref, o_ref, state_ref):
    seq = q_ref.shape[1]
    groups = state_ref.shape[0]
    dk = q_ref.shape[2] // groups
    dv = v_ref.shape[2] // groups

    @pl.when(pl.program_id(2) == 0)
    def _():
        state_ref[...] = jnp.zeros_like(state_ref)

    tri = _tril_ones(CHUNK)
    row = lax.broadcasted_iota(jnp.int32, (CHUNK, CHUNK), 0)
    col = lax.broadcasted_iota(jnp.int32, (CHUNK, CHUNK), 1)

    def body(n, carry):
        r = pl.ds(pl.multiple_of(n * CHUNK, CHUNK), CHUNK)
        logit = _fdot(gl_ref[0, r, :], wg_ref[...]) + gb_ref[...]
        log_alpha = -_softplus(-logit) / GLA_TAU
        cum_all = _fdot(tri, log_alpha)
        for h in range(groups):
            kc = slice(h * dk, (h + 1) * dk)
            vc = slice(h * dv, (h + 1) * dv)
            cum = cum_all[:, kc]
            ref = cum[CHUNK // 2 - 1:CHUNK // 2, :]
            last = cum[CHUNK - 1:CHUNK, :]
            fwd, bwd = jnp.exp(cum - ref), jnp.exp(ref - cum)
            q = q_ref[0, r, kc].astype(F32) * dk ** -0.5
            k = k_ref[0, r, kc].astype(F32)
            v = v_ref[0, r, vc]
            s_lo = _bdot_nt(q * fwd, k * bwd)
            s_up = _bdot_nt(q * bwd, k * fwd)
            scores = jnp.where(row >= col, s_lo, s_up)
            state = state_ref[h]
            o = _bdot(scores, v) + _bdot_nt(q * jnp.exp(cum), state)
            state_ref[h] = state * jnp.exp(last) + _bdot_tn(v, k * jnp.exp(last - cum))
            o = _rms(o, gain_ref[h])
            o_ref[0, r, vc] = (o * _silu(r_ref[0, r, vc].astype(F32))).astype(o_ref.dtype)
        return carry

    lax.fori_loop(0, seq // CHUNK, body, 0, unroll=CHUNK_UNROLL)


def _gla_core(proj, gate_low, w_gate_up, gate_bias, norm_gain, batch, seq):
    h = GLA_HEADS
    grp = GLA_GROUP
    ng = h // grp
    dv = norm_gain.shape[-1]
    dk = dv // 2
    p3 = proj.reshape(batch, seq, -1)
    gl3 = gate_low.reshape(batch, seq, LANES)
    wg = jnp.zeros((LANES, h * dk), F32).at[:GLA_GATE_RANK].set(w_gate_up)
    sb = min(SEQ_BLOCK, seq)
    return pl.pallas_call(
        _gla_kernel, grid=(batch, ng, seq // sb),
        in_specs=[pl.BlockSpec((1, sb, grp * dk), lambda b, i, t: (b, t, i)),
                  pl.BlockSpec((1, sb, grp * dk), lambda b, i, t: (b, t, ng + i)),
                  pl.BlockSpec((1, sb, grp * dv), lambda b, i, t: (b, t, ng + i)),
                  pl.BlockSpec((1, sb, grp * dv), lambda b, i, t: (b, t, 2 * ng + i)),
                  pl.BlockSpec((1, sb, LANES), lambda b, i, t: (b, t, 0)),
                  pl.BlockSpec((LANES, grp * dk), lambda b, i, t: (0, i)),
                  pl.BlockSpec((1, grp * dk), lambda b, i, t: (0, i)),
                  pl.BlockSpec((grp, 1, dv), lambda b, i, t: (i, 0, 0))],
        out_specs=pl.BlockSpec((1, sb, grp * dv), lambda b, i, t: (b, t, i)),
        out_shape=jax.ShapeDtypeStruct((batch, seq, h * dv), BF16),
        scratch_shapes=[pltpu.VMEM((grp, dv, dk), F32)],
        compiler_params=_cparams(("parallel", "parallel", "arbitrary")), name="gla_core",
    )(p3, p3, p3, p3, gl3, wg, gate_bias.reshape(1, h * dk), norm_gain.reshape(h, 1, dv))


def _causal_conv(x, w):
    width = w.shape[0]
    xp = jnp.concatenate([jnp.zeros((SUBLANES, x.shape[1]), x.dtype), x], axis=0)
    out = x * w[width - 1:width, :]
    for sh in range(1, width):
        out = out + pltpu.roll(xp, sh, axis=0)[SUBLANES:] * w[width - 1 - sh:width - sh, :]
    return out


def _l2norm(x):
    return x * lax.rsqrt(jnp.sum(x * x, axis=-1, keepdims=True) + NORM_EPS)


def _gdn_gate_kernel(bl_ref, al_ref, alog_ref, dt_ref, beta_ref, cum_ref):
    beta_ref[0] = jax.nn.sigmoid(bl_ref[0])
    log_alpha = -jnp.exp(alog_ref[...]) * _softplus(al_ref[0] + dt_ref[...])
    cum_ref[0] = _fdot(_tril_ones(CHUNK), log_alpha)


def _gdn_gates(gates, a_log, dt_bias, batch, seq):
    h = GDN_HEADS
    n = seq // CHUNK
    g4 = gates.reshape(batch, n, CHUNK, LANES)

    def to_pos_major(x):
        return x.transpose(0, 2, 3, 1).reshape(batch, CHUNK, h * n)

    spec = pl.BlockSpec((1, CHUNK, h * n), lambda b: (b, 0, 0))
    tab = pl.BlockSpec((1, h * n), lambda b: (0, 0))
    return pl.pallas_call(
        _gdn_gate_kernel, grid=(batch,),
        in_specs=[spec, spec, tab, tab], out_specs=[spec, spec],
        out_shape=[jax.ShapeDtypeStruct((batch, CHUNK, h * n), F32)] * 2,
        compiler_params=_cparams(("parallel",)), name="gdn_gates",
    )(to_pos_major(g4[..., :h]), to_pos_major(g4[..., h:2 * h]),
      jnp.repeat(a_log.astype(F32), n).reshape(1, h * n),
      jnp.repeat(dt_bias.astype(F32), n).reshape(1, h * n))


def _unit_lower_inverse_minus_eye(a, row, col):
    def blk(size):
        return (row // size) == (col // size)

    d = jnp.where(blk(8), a, 0.0)
    d2 = _bmm(d, d)
    d3 = _bmm(d2, d)
    d4 = _bmm(d2, d2)
    q = d2 - d - d3
    y = q + d4 + _bmm(q, d4)
    for size in (16, 32, 64):
        e = jnp.where(blk(size) & jnp.logical_not(blk(size // 2)), a, 0.0)
        t = e + _bmm(y, e)
        y = y - t - _bmm(t, y)
    return y


def _lane_column(x, lane_ids, idx):
    return jnp.sum(jnp.where(lane_ids == idx, x, 0.0), axis=1, keepdims=True)


def _gdn_kernel(q_ref, k_ref, v_ref, z_ref, wq_ref, wk_ref, wv_ref, betat_ref, cumt_ref, cumr_ref,
                gain_ref, o_ref, q_s, k_s, v_s, m_s, g_s, sb_s, st_s):
    seq = q_ref.shape[1]
    groups = st_s.shape[0]
    dk = st_s.shape[1]
    n_chunks = seq // CHUNK
    ub = min(GDN_SOLVE_BATCH, n_chunks)
    row = lax.broadcasted_iota(jnp.int32, (CHUNK, CHUNK), 0)
    col = lax.broadcasted_iota(jnp.int32, (CHUNK, CHUNK), 1)
    strict = row > col
    betat = betat_ref[0]
    cumt = cumt_ref[0]
    lane_ids = lax.broadcasted_iota(jnp.int32, cumt.shape, 1)
    dec_all = jnp.exp(cumt[CHUNK - 1:CHUNK, :])
    gain = gain_ref[...]

    for g in range(groups):
        cs = slice(g * dk, (g + 1) * dk)
        qg = _l2norm(_silu(_causal_conv(q_ref[0, :, cs].astype(F32), wq_ref[:, cs])))
        q_s[g] = (qg * dk ** -0.5).astype(q_s.dtype)
        k_s[g] = _l2norm(_silu(_causal_conv(k_ref[0, :, cs].astype(F32), wk_ref[:, cs])))
        v_s[g] = _silu(_causal_conv(v_ref[0, :, cs].astype(F32), wv_ref[:, cs]))

    def solve(g, n0):
        r = pl.ds(pl.multiple_of(n0 * CHUNK, ub * CHUNK), ub * CHUNK)
        k = k_s[g, r, :].reshape(ub, CHUNK, dk)
        v = v_s[g, r, :].reshape(ub, CHUNK, dk)
        cum_c = jnp.stack([_lane_column(cumt, lane_ids, g * n_chunks + n0 + u) for u in range(ub)])
        beta = jnp.stack([_lane_column(betat, lane_ids, g * n_chunks + n0 + u) for u in range(ub)])
        cum_r = cumr_ref[0, g, pl.ds(n0, ub)]
        rel = jnp.where(strict, jnp.exp(jnp.where(strict, cum_c - cum_r, 0.0)), 0.0)
        a = beta * rel * _bmm_nt(k, k)
        y = _unit_lower_inverse_minus_eye(a, row, col)
        rhs = jnp.concatenate([beta * v, (beta * jnp.exp(cum_c)) * k], axis=-1)
        sol = rhs + _bmm(y, rhs)
        ke = k * jnp.exp(cum_c[:, CHUNK - 1:CHUNK, :] - cum_c)
        gm = _bmm_tn(ke, sol)
        rs = pl.ds(pl.multiple_of(n0 * dk, ub * dk), ub * dk)
        g_s[g, rs, :] = gm[..., :dk].reshape(ub * dk, dk)
        m_s[g, rs, :] = gm[..., dk:].reshape(ub * dk, dk).astype(m_s.dtype)

    def solve_body(i, carry):
        for g in range(groups):
            solve(g, i * ub)
        return carry

    lax.fori_loop(0, n_chunks // ub, solve_body, 0)

    st_s[...] = jnp.zeros_like(st_s)

    def state_body(n, carry):
        rs = pl.ds(pl.multiple_of(n * dk, dk), dk)
        for g in range(groups):
            dec = _lane_column(dec_all, lane_ids[:1], g * n_chunks + n)
            state = st_s[g]
            state = state * dec - _bdot(m_s[g, rs, :], state) + g_s[g, rs, :]
            st_s[g] = state
            sb_s[g, rs, :] = state.astype(sb_s.dtype)
        return carry

    lax.fori_loop(0, n_chunks, state_body, 0)

    def readout(g, n0):
        r = pl.ds(pl.multiple_of(n0 * CHUNK, ub * CHUNK), ub * CHUNK)
        rs = pl.ds(pl.multiple_of(n0 * dk, ub * dk), ub * dk)
        cs = slice(g * dk, (g + 1) * dk)
        o = _bmm(q_s[g, r, :].reshape(ub, CHUNK, dk), sb_s[g, rs, :].reshape(ub, dk, dk))
        o = _rms(o.reshape(ub * CHUNK, dk), gain)
        o_ref[0, r, cs] = (o * _silu(z_ref[0, r, cs].astype(F32))).astype(o_ref.dtype)

    def readout_body(i, carry):
        for g in range(groups):
            readout(g, i * ub)
        return carry

    lax.fori_loop(0, n_chunks // ub, readout_body, 0)


def _gdn_core(proj, gates, conv_w, a_log, dt_bias, norm_gain, batch, seq):
    h = GDN_HEADS
    grp = GDN_GROUP
    ng = h // grp
    dk = norm_gain.shape[-1]
    n = seq // CHUNK
    p3 = proj.reshape(batch, seq, -1)
    beta_t, cum_t = _gdn_gates(gates, a_log, dt_bias, batch, seq)
    cum_r = cum_t.reshape(batch, CHUNK, h, n).transpose(0, 2, 3, 1)[:, :, :, None, :]
    head = lambda off: pl.BlockSpec((1, seq, grp * dk), lambda b, i: (b, 0, off + i))
    convw = lambda off: pl.BlockSpec((CONV_WIDTH, grp * dk), lambda b, i: (0, off + i))
    gate_spec = pl.BlockSpec((1, CHUNK, grp * n), lambda b, i: (b, 0, i))
    per_head = lambda dt: pltpu.VMEM((grp, seq, dk), dt)
    per_chunk = lambda dt: pltpu.VMEM((grp, n * dk, dk), dt)
    return pl.pallas_call(
        _gdn_kernel, grid=(batch, ng),
        in_specs=[head(0), head(ng), head(2 * ng), head(3 * ng),
                  convw(0), convw(ng), convw(2 * ng),
                  gate_spec, gate_spec,
                  pl.BlockSpec((1, grp, n, 1, CHUNK), lambda b, i: (b, i, 0, 0, 0)),
                  pl.BlockSpec((1, dk), lambda b, i: (0, 0))],
        out_specs=pl.BlockSpec((1, seq, grp * dk), lambda b, i: (b, 0, i)),
        out_shape=jax.ShapeDtypeStruct((batch, seq, h * dk), BF16),
        scratch_shapes=[per_head(BF16), per_head(F32), per_head(F32), per_chunk(BF16), per_chunk(F32),
                        per_chunk(BF16), pltpu.VMEM((grp, dk, dk), F32)],
        compiler_params=_cparams(("parallel", "parallel")), name="gdn_core",
    )(p3, p3, p3, p3, conv_w, conv_w, conv_w, beta_t, cum_t, cum_r, norm_gain.reshape(1, dk))


def _lru_kernel(x_ref, y_ref, cw_ref, cb_ref, w_ref, br_ref, bi_ref, lam_ref, o_ref, a_s, u_s):
    seq = x_ref.shape[1]
    width = x_ref.shape[2]
    x = _causal_conv(x_ref[0].astype(F32), cw_ref[...]) + cb_ref[...]
    gates = _bdot(x, w_ref[0])
    r = jax.nn.sigmoid(gates[:, :width] + br_ref[...])
    i = jax.nn.sigmoid(gates[:, width:] + bi_ref[...])
    a = jnp.exp((-LRU_C * _softplus(-lam_ref[...])) * r)
    a_s[...] = a
    t = 1.0 - a * a
    u_s[...] = jnp.where(t > 0.0, t * lax.rsqrt(t), 0.0) * (i * x)
    row = lax.broadcasted_iota(jnp.int32, (SUBLANES, width), 0)

    def scan_block(t, h_prev):
        rows = pl.ds(pl.multiple_of(t * SUBLANES, SUBLANES), SUBLANES)
        a = a_s[rows, :]
        u = u_s[rows, :]
        for d in (1, 2, 4):
            m = row >= d
            u = jnp.where(m, a * pltpu.roll(u, d, axis=0) + u, u)
            a = jnp.where(m, a * pltpu.roll(a, d, axis=0), a)
        h = u + a * h_prev
        y = jax.nn.gelu(y_ref[0, rows, :].astype(F32))
        o_ref[0, rows, :] = (h * y).astype(o_ref.dtype)
        return jnp.broadcast_to(h[SUBLANES - 1:SUBLANES, :], h.shape)

    unroll = 8

    def body(j, h_prev):
        for s in range(unroll):
            h_prev = scan_block(j * unroll + s, h_prev)
        return h_prev

    lax.fori_loop(0, seq // (SUBLANES * unroll), body, jnp.zeros((SUBLANES, width), F32))


def _lru_core(proj, conv_w, conv_b, w_r, b_r, w_i, b_i, lam, batch, seq):
    width = conv_w.shape[-1]
    nb = width // LRU_BLOCK
    p3 = proj.reshape(batch, seq, -1)
    w_ri = jnp.concatenate([w_r, w_i], axis=-1).astype(BF16)
    vec = lambda: pl.BlockSpec((1, LRU_BLOCK), lambda b, j: (0, j))
    return pl.pallas_call(
        _lru_kernel, grid=(batch, nb),
        in_specs=[pl.BlockSpec((1, seq, LRU_BLOCK), lambda b, j: (b, 0, j)),
                  pl.BlockSpec((1, seq, LRU_BLOCK), lambda b, j: (b, 0, nb + j)),
                  pl.BlockSpec((CONV_WIDTH, LRU_BLOCK), lambda b, j: (0, j)),
                  vec(),
                  pl.BlockSpec((1, LRU_BLOCK, 2 * LRU_BLOCK), lambda b, j: (j, 0, 0)),
                  vec(), vec(), vec()],
        out_specs=pl.BlockSpec((1, seq, LRU_BLOCK), lambda b, j: (b, 0, j)),
        out_shape=jax.ShapeDtypeStruct((batch, seq, width), BF16),
        scratch_shapes=[pltpu.VMEM((seq, LRU_BLOCK), F32), pltpu.VMEM((seq, LRU_BLOCK), F32)],
        compiler_params=_cparams(("parallel", "parallel")), name="lru_core",
    )(p3, p3, conv_w, conv_b.reshape(1, width), w_ri, b_r.reshape(1, width), b_i.reshape(1, width),
      lam.reshape(1, width))


def kernel(x, norm1, norm2, final_norm, ret_w_in, ret_gn_gain, ret_w_out, gdn_w_in, gdn_conv_w, gdn_a_log, gdn_dt_bias, gdn_norm_gain, gdn_w_out, gla_w_in, gla_w_gate_up, gla_gate_bias, gla_norm_gain, gla_w_out, lru_w_in, lru_conv_w, lru_conv_b, lru_w_rgate, lru_b_rgate, lru_w_igate, lru_b_igate, lru_lambda, lru_w_out, mlp_w_up, mlp_w_down):
    batch, seq, d = x.shape
    depth = norm1.shape[0]
    xf = x.reshape(batch * seq, d)
    hn = _rmsnorm(xf, norm1[0], BF16)
    w_out_all = (ret_w_out, gdn_w_out, gla_w_out, lru_w_out)
    wu, wd, wo = mlp_w_up[0].astype(BF16), mlp_w_down[0].astype(BF16), ret_w_out[0].astype(BF16)
    for layer in range(depth):
        m, j = layer % 4, layer // 4
        if m == 0:
            w_in = ret_w_in[j]
            proj = _in_proj(hn, w_in, w_in.shape[1], BF16, "ret_in")
            o = _retention_core(proj, ret_gn_gain[j], batch, seq)
        elif m == 1:
            w_t = gdn_w_in[j].T
            n_main = w_t.shape[0] - 2 * GDN_HEADS
            proj = _in_proj(hn, w_t, n_main, BF16, "gdn_in", w_transposed=True)
            gates = _gate_proj(hn, w_t, n_main, 2 * GDN_HEADS, "gdn_gate_in")
            o = _gdn_core(proj, gates, gdn_conv_w[j], gdn_a_log[j], gdn_dt_bias[j], gdn_norm_gain[j], batch, seq)
        elif m == 2:
            w_t = gla_w_in[j].T
            n_main = w_t.shape[0] - GLA_GATE_RANK
            proj = _in_proj(hn, w_t, n_main, BF16, "gla_in", w_transposed=True)
            gate_low = _gate_proj(hn, w_t, n_main, GLA_GATE_RANK, "gla_gate_in")
            o = _gla_core(proj, gate_low, gla_w_gate_up[j], gla_gate_bias[j], gla_norm_gain[j], batch, seq)
        else:
            w_in = lru_w_in[j]
            proj = _in_proj(hn, w_in, w_in.shape[1], BF16, "lru_in")
            o = _lru_core(proj, lru_conv_w[j], lru_conv_b[j], lru_w_rgate[j], lru_b_rgate[j].reshape(-1),
                          lru_w_igate[j], lru_b_igate[j].reshape(-1), lru_lambda[j], batch, seq)
        xf, hn = _proj_res_norm(o.reshape(batch * seq, -1), wo, xf, norm2[layer])
        if layer + 1 < depth:
            nxt = layer + 1
            xf, hn, wu, wd, wo = _mlp(hn, wu, wd, xf, norm1[nxt], True, BF16,
                                      next_w=((mlp_w_up, nxt), (mlp_w_down, nxt), (w_out_all[nxt % 4], nxt // 4)))
        else:
            (out,) = _mlp(hn, wu, wd, xf, final_norm, False, x.dtype)
    return out.reshape(batch, seq, d)
```

```python
import functools

import jax
import jax.numpy as jnp
from jax import lax
from jax.experimental import pallas as pl
from jax.experimental.pallas import tpu as pltpu

F32 = jnp.float32
BF16 = jnp.bfloat16

CHUNK = 64
NORM_EPS = 1e-6
ROPE_BASE = 10000.0
RET_HEADS = 8
GDN_HEADS = 16
GLA_HEADS = 4
GLA_GATE_RANK = 16
GLA_TAU = 16.0
LRU_BLOCK = 128
LRU_C = 8.0
CONV_WIDTH = 4

LANES = 128
SUBLANES = 8
VMEM_LIMIT = 56 * 1024 * 1024

IN_PROJ_TILE = (1024, 1024)
OUT_PROJ_ROWS = 512
MLP_TILE = (512, 1024)
RET_GROUP = 4
GLA_GROUP = 4
CHUNK_UNROLL = 2
SEQ_BLOCK = 512
GDN_GROUP = 4
GDN_SOLVE_BATCH = 16


def _cparams(sem):
    return pltpu.CompilerParams(dimension_semantics=sem, vmem_limit_bytes=VMEM_LIMIT)


def _rms(x, g):
    return x * lax.rsqrt(jnp.mean(x * x, axis=-1, keepdims=True) + NORM_EPS) * g


def _softplus(x):
    return jnp.maximum(x, 0.0) + jnp.log1p(jnp.exp(-jnp.abs(x)))


def _silu(x):
    return x * jax.nn.sigmoid(x)


def _bdot(a, b):
    return jnp.dot(a.astype(BF16), b.astype(BF16), preferred_element_type=F32)


def _bdot_nt(a, b):
    return lax.dot_general(a.astype(BF16), b.astype(BF16), (((1,), (1,)), ((), ())),
                           preferred_element_type=F32)


def _bdot_tn(a, b):
    return lax.dot_general(a.astype(BF16), b.astype(BF16), (((0,), (0,)), ((), ())),
                           preferred_element_type=F32)


def _bmm(a, b):
    return jnp.einsum("uij,ujk->uik", a.astype(BF16), b.astype(BF16), preferred_element_type=F32)


def _bmm_nt(a, b):
    return jnp.einsum("uid,ujd->uij", a.astype(BF16), b.astype(BF16), preferred_element_type=F32)


def _bmm_tn(a, b):
    return jnp.einsum("ucd,uce->ude", a.astype(BF16), b.astype(BF16), preferred_element_type=F32)


def _fdot(a, b):
    return jnp.dot(a, b, precision=lax.Precision.HIGHEST, preferred_element_type=F32)


def _tril_ones(n):
    r = lax.broadcasted_iota(jnp.int32, (n, n), 0)
    c = lax.broadcasted_iota(jnp.int32, (n, n), 1)
    return (r >= c).astype(F32)


def _norm_kernel(x_ref, g_ref, o_ref):
    o_ref[...] = _rms(x_ref[...], g_ref[...]).astype(o_ref.dtype)


def _rmsnorm(x, g, out_dtype, tm=512):
    m, d = x.shape
    return pl.pallas_call(
        _norm_kernel, grid=(m // tm,),
        in_specs=[pl.BlockSpec((tm, d), lambda i: (i, 0)), pl.BlockSpec((1, d), lambda i: (0, 0))],
        out_specs=pl.BlockSpec((tm, d), lambda i: (i, 0)),
        out_shape=jax.ShapeDtypeStruct((m, d), out_dtype),
        compiler_params=_cparams(("parallel",)), name="rmsnorm")(x, g.reshape(1, d))


def _ride_along(next_w, n_slabs, slab_index):
    in_specs, out_specs, out_shape, args = [], [], [], []
    for w_all, layer in next_w:
        _, rows, cols = w_all.shape
        in_specs.append(pl.BlockSpec((None, rows // n_slabs, cols),
                                     lambda *ids, l=layer: (l, slab_index(*ids), 0)))
        out_specs.append(pl.BlockSpec((rows // n_slabs, cols), lambda *ids: (slab_index(*ids), 0)))
        out_shape.append(jax.ShapeDtypeStruct((rows, cols), BF16))
        args.append(w_all)
    return in_specs, out_specs, out_shape, args


def _in_proj_kernel(a_ref, w_ref, *refs, w_transposed, convert):
    f32_refs, o_ref, bf16_refs, wb_ref = refs[:convert], refs[convert], refs[convert + 1:-1], refs[-1]

    @pl.when(pl.program_id(1) == 0)
    def _():
        wb_ref[...] = w_ref[...].astype(wb_ref.dtype)

    dims = (((1,), (1,)), ((), ())) if w_transposed else (((1,), (0,)), ((), ()))
    o_ref[...] = lax.dot_general(a_ref[...], wb_ref[...], dims, preferred_element_type=F32).astype(o_ref.dtype)
    for src, dst in zip(f32_refs, bf16_refs):
        dst[...] = src[...].astype(dst.dtype)


def _in_proj(a, w, n_cols, out_dtype, name, w_transposed=False, next_w=()):
    m, k = a.shape
    tm, tn = IN_PROJ_TILE
    tm, tn = min(tm, m), min(tn, n_cols)
    nj, ni = n_cols // tn, m // tm
    if w_transposed:
        w_spec = pl.BlockSpec((tn, k), lambda j, i: (j, 0))
        w_scratch = pltpu.VMEM((tn, k), BF16)
    else:
        w_spec = pl.BlockSpec((k, tn), lambda j, i: (0, j))
        w_scratch = pltpu.VMEM((k, tn), BF16)
    n_slabs = 1 << ((nj * ni).bit_length() - 1)
    ra_in, ra_out, ra_shape, ra_args = _ride_along(
        next_w, n_slabs, lambda j, i: jnp.minimum(j * ni + i, n_slabs - 1))
    res = pl.pallas_call(
        functools.partial(_in_proj_kernel, w_transposed=w_transposed, convert=len(next_w)), grid=(nj, ni),
        in_specs=[pl.BlockSpec((tm, k), lambda j, i: (i, 0)), w_spec] + ra_in,
        out_specs=[pl.BlockSpec((tm, tn), lambda j, i: (i, j))] + ra_out,
        out_shape=[jax.ShapeDtypeStruct((m, n_cols), out_dtype)] + ra_shape,
        scratch_shapes=[w_scratch],
        compiler_params=_cparams(("arbitrary" if next_w else "parallel", "arbitrary")),
        name=name)(a, w, *ra_args)
    return res if next_w else res[0]


def _gate_proj_kernel(a_ref, w_ref, o_ref):
    n = w_ref.shape[0]
    res = lax.dot_general(a_ref[...], w_ref[...].astype(BF16), (((1,), (1,)), ((), ())),
                          preferred_element_type=F32)
    o_ref[...] = jnp.zeros_like(o_ref)
    o_ref[:, :n] = res


def _gate_proj(a, w_t, row0, n_rows, name, tm=1024):
    m, k = a.shape
    tm = min(tm, m)
    return pl.pallas_call(
        _gate_proj_kernel, grid=(m // tm,),
        in_specs=[pl.BlockSpec((tm, k), lambda i: (i, 0)),
                  pl.BlockSpec((n_rows, k), lambda i: (row0 // n_rows, 0))],
        out_specs=pl.BlockSpec((tm, LANES), lambda i: (i, 0)),
        out_shape=jax.ShapeDtypeStruct((m, LANES), F32),
        compiler_params=_cparams(("parallel",)), name=name)(a, w_t)


def _proj_res_norm_kernel(a_ref, w_ref, x_ref, g_ref, xo_ref, ho_ref):
    x = x_ref[...] + jnp.dot(a_ref[...], w_ref[...], preferred_element_type=F32)
    xo_ref[...] = x
    ho_ref[...] = _rms(x, g_ref[...]).astype(ho_ref.dtype)


def _proj_res_norm(a, w, x, g):
    m, k = a.shape
    d = w.shape[1]
    tm = min(OUT_PROJ_ROWS, m)
    return pl.pallas_call(
        _proj_res_norm_kernel, grid=(m // tm,),
        in_specs=[pl.BlockSpec((tm, k), lambda i: (i, 0)),
                  pl.BlockSpec((k, d), lambda i: (0, 0), pipeline_mode=pl.Buffered(1)),
                  pl.BlockSpec((tm, d), lambda i: (i, 0)),
                  pl.BlockSpec((1, d), lambda i: (0, 0))],
        out_specs=[pl.BlockSpec((tm, d), lambda i: (i, 0)), pl.BlockSpec((tm, d), lambda i: (i, 0))],
        out_shape=[jax.ShapeDtypeStruct((m, d), F32), jax.ShapeDtypeStruct((m, d), BF16)],
        compiler_params=_cparams(("parallel",)), name="proj_res_norm")(a, w, x, g.reshape(1, d))


def _mlp_kernel(*refs, write_x, convert):
    h_ref, wu_ref, wd_ref, x_ref, g_ref = refs[:5]
    refs = refs[5:]
    f32_refs, refs = refs[:convert], refs[convert:]
    if write_x:
        xo_ref = refs[0]
        refs = refs[1:]
    ho_ref = refs[0]
    refs = refs[1:]
    bf16_refs, refs = refs[:convert], refs[convert:]
    (acc_ref,) = refs
    s = pl.program_id(1)

    @pl.when(s == 0)
    def _():
        acc_ref[...] = x_ref[...]

    u = jnp.dot(h_ref[...], wu_ref[...], preferred_element_type=F32)
    u = jnp.square(jnp.maximum(u, 0.0)).astype(BF16)
    acc_ref[...] += jnp.dot(u, wd_ref[...], preferred_element_type=F32)

    @pl.when(s == pl.num_programs(1) - 1)
    def _():
        x = acc_ref[...]
        if write_x:
            xo_ref[...] = x
        ho_ref[...] = _rms(x, g_ref[...]).astype(ho_ref.dtype)

    for src, dst in zip(f32_refs, bf16_refs):
        dst[...] = src[...].astype(dst.dtype)


def _mlp(h, wu, wd, x, g, write_x, h_dtype, next_w=()):
    m, d = h.shape
    ff = wu.shape[1]
    tm, tf = MLP_TILE
    tm, tf = min(tm, m), min(tf, ff)
    n_ff = ff // tf
    ff_block = lambda i, s: jnp.where(i % 2 == 0, s, n_ff - 1 - s)
    row = pl.BlockSpec((tm, d), lambda i, s: (i, 0))
    in_specs = [row,
                pl.BlockSpec((d, tf), lambda i, s: (0, ff_block(i, s))),
                pl.BlockSpec((tf, d), lambda i, s: (ff_block(i, s), 0)),
                row,
                pl.BlockSpec((1, d), lambda i, s: (0, 0))]
    args = [h, wu, wd, x, g.reshape(1, d)]
    out_specs = [row]
    out_shape = [jax.ShapeDtypeStruct((m, d), h_dtype)]
    if write_x:
        out_specs = [row, row]
        out_shape = [jax.ShapeDtypeStruct((m, d), F32)] + out_shape
    ra_in, ra_out, ra_shape, ra_args = _ride_along(next_w, (m // tm) * n_ff, lambda i, s: i * n_ff + s)
    in_specs, out_specs, out_shape, args = in_specs + ra_in, out_specs + ra_out, out_shape + ra_shape, args + ra_args
    return pl.pallas_call(
        functools.partial(_mlp_kernel, write_x=write_x, convert=len(next_w)),
        grid=(m // tm, n_ff),
        in_specs=in_specs, out_specs=out_specs, out_shape=out_shape,
        scratch_shapes=[pltpu.VMEM((tm, d), F32)],
        compiler_params=_cparams(("parallel", "arbitrary")), name="mlp")(*args)


def _rotary(x, c, s):
    half = x.shape[-1] // 2
    x1, x2 = x[:, :half], x[:, half:]
    return jnp.concatenate([x1 * c - x2 * s, x1 * s + x2 * c], axis=-1)


def _ret_kernel(q_ref, k_ref, v_ref, g_ref, cos_ref, sin_ref, dmat_ref, qd_ref, kd_ref, cd_ref,
                gain_ref, o_ref, state_ref):
    seq = q_ref.shape[1]
    groups = state_ref.shape[0]
    dk = q_ref.shape[2] // groups
    dv = v_ref.shape[2] // groups

    @pl.when(pl.program_id(2) == 0)
    def _():
        state_ref[...] = jnp.zeros_like(state_ref)

    def body(n, carry):
        r = pl.ds(pl.multiple_of(n * CHUNK, CHUNK), CHUNK)
        c = cos_ref[r, :]
        s = sin_ref[r, :]
        for h in range(groups):
            kc = slice(h * dk, (h + 1) * dk)
            vc = slice(h * dv, (h + 1) * dv)
            q = _rotary(q_ref[0, r, kc].astype(F32), c, s)
            k = _rotary(k_ref[0, r, kc].astype(F32), c, s) * dk ** -0.5
            v = v_ref[0, r, vc]
            scores = _bdot_nt(q, k) * dmat_ref[h]
            state = state_ref[h]
            o = _bdot(scores, v) + _bdot(q * qd_ref[h], state)
            state_ref[h] = state * cd_ref[h] + _bdot_tn(k * kd_ref[h], v)
            o = o - jnp.mean(o, axis=-1, keepdims=True)
            o = _rms(o, gain_ref[h])
            o_ref[0, r, vc] = (o * _silu(g_ref[0, r, vc].astype(F32))).astype(o_ref.dtype)
        return carry

    lax.fori_loop(0, seq // CHUNK, body, 0, unroll=CHUNK_UNROLL)


def _retention_core(proj, gn_gain, batch, seq):
    h = RET_HEADS
    grp = RET_GROUP
    ng = h // grp
    dv = gn_gain.shape[-1]
    dk = dv // 2
    p3 = proj.reshape(batch, seq, -1)
    inv = ROPE_BASE ** (-jnp.arange(0, dk, 2, dtype=F32) / dk)
    ang = jnp.arange(seq, dtype=F32)[:, None] * inv[None, :]
    cos, sin = jnp.cos(ang), jnp.sin(ang)
    log_gamma = jnp.log1p(-jnp.exp2(-5.0 - jnp.arange(h, dtype=F32)))
    pos = jnp.arange(CHUNK, dtype=F32)
    dist = jnp.abs(pos[:, None] - pos[None, :])
    dmat = jnp.exp(log_gamma[:, None, None] * dist)
    qd = jnp.exp(log_gamma[:, None] * (pos + 1.0))[:, :, None]
    kd = jnp.exp(log_gamma[:, None] * (CHUNK - 1.0 - pos))[:, :, None]
    cd = jnp.exp(log_gamma * CHUNK)[:, None, None]
    sb = min(SEQ_BLOCK, seq)
    per_head = lambda shape: pl.BlockSpec((grp,) + shape, lambda b, i, t: (i, 0, 0))
    return pl.pallas_call(
        _ret_kernel, grid=(batch, ng, seq // sb),
        in_specs=[pl.BlockSpec((1, sb, grp * dk), lambda b, i, t: (b, t, i)),
                  pl.BlockSpec((1, sb, grp * dk), lambda b, i, t: (b, t, ng + i)),
                  pl.BlockSpec((1, sb, grp * dv), lambda b, i, t: (b, t, ng + i)),
                  pl.BlockSpec((1, sb, grp * dv), lambda b, i, t: (b, t, 2 * ng + i)),
                  pl.BlockSpec((sb, dk // 2), lambda b, i, t: (t, 0)),
                  pl.BlockSpec((sb, dk // 2), lambda b, i, t: (t, 0)),
                  per_head((CHUNK, CHUNK)), per_head((CHUNK, 1)), per_head((CHUNK, 1)),
                  per_head((1, 1)), per_head((1, dv))],
        out_specs=pl.BlockSpec((1, sb, grp * dv), lambda b, i, t: (b, t, i)),
        out_shape=jax.ShapeDtypeStruct((batch, seq, h * dv), BF16),
        scratch_shapes=[pltpu.VMEM((grp, dk, dv), F32)],
        compiler_params=_cparams(("parallel", "parallel", "arbitrary")), name="retention_core",
    )(p3, p3, p3, p3, cos, sin, dmat, qd, kd, cd, gn_gain.reshape(h, 1, dv))


def _gla_kernel(q_ref, k_ref, v_ref, r_ref, gl_ref, wg_ref, gb_ref, gain_ref, o_ref, state_ref):
    seq = q_ref.shape[1]
    groups = state_ref.shape[0]
    dk = q_ref.shape[2] // groups
    dv = v_ref.shape[2] // groups

    @pl.when(pl.program_id(2) == 0)
    def _():
        state_ref[...] = jnp.zeros_like(state_ref)

    tri = _tril_ones(CHUNK)
    row = lax.broadcasted_iota(jnp.int32, (CHUNK, CHUNK), 0)
    col = lax.broadcasted_iota(jnp.int32, (CHUNK, CHUNK), 1)

    def body(n, carry):
        r = pl.ds(pl.multiple_of(n * CHUNK, CHUNK), CHUNK)
        logit = _fdot(gl_ref[0, r, :], wg_ref[...]) + gb_ref[...]
        log_alpha = -_softplus(-logit) / GLA_TAU
        cum_all = _fdot(tri, log_alpha)
        for h in range(groups):
            kc = slice(h * dk, (h + 1) * dk)
            vc = slice(h * dv, (h + 1) * dv)
            cum = cum_all[:, kc]
            ref = cum[CHUNK // 2 - 1:CHUNK // 2, :]
            last = cum[CHUNK - 1:CHUNK, :]
            fwd, bwd = jnp.exp(cum - ref), jnp.exp(ref - cum)
            q = q_ref[0, r, kc].astype(F32) * dk ** -0.5
            k = k_ref[0, r, kc].astype(F32)
            v = v_ref[0, r, vc]
            s_lo = _bdot_nt(q * fwd, k * bwd)
            s_up = _bdot_nt(q * bwd, k * fwd)
            scores = jnp.where(row >= col, s_lo, s_up)
            state = state_ref[h]
            o = _bdot(scores, v) + _bdot_nt(q * jnp.exp(cum), state)
            state_ref[h] = state * jnp.exp(last) + _bdot_tn(v, k * jnp.exp(last - cum))
            o = _rms(o, gain_ref[h])
            o_ref[0, r, vc] = (o * _silu(r_ref[0, r, vc].astype(F32))).astype(o_ref.dtype)
        return carry

    lax.fori_loop(0, seq // CHUNK, body, 0, unroll=CHUNK_UNROLL)


def _gla_core(proj, gate_low, w_gate_up, gate_bias, norm_gain, batch, seq):
    h = GLA_HEADS
    grp = GLA_GROUP
    ng = h // grp
    dv = norm_gain.shape[-1]
    dk = dv // 2
    p3 = proj.reshape(batch, seq, -1)
    gl3 = gate_low.reshape(batch, seq, LANES)
    wg = jnp.zeros((LANES, h * dk), F32).at[:GLA_GATE_RANK].set(w_gate_up)
    sb = min(SEQ_BLOCK, seq)
    return pl.pallas_call(
        _gla_kernel, grid=(batch, ng, seq // sb),
        in_specs=[pl.BlockSpec((1, sb, grp * dk), lambda b, i, t: (b, t, i)),
                  pl.BlockSpec((1, sb, grp * dk), lambda b, i, t: (b, t, ng + i)),
                  pl.BlockSpec((1, sb, grp * dv), lambda b, i, t: (b, t, ng + i)),
                  pl.BlockSpec((1, sb, grp * dv), lambda b, i, t: (b, t, 2 * ng + i)),
                  pl.BlockSpec((1, sb, LANES), lambda b, i, t: (b, t, 0)),
                  pl.BlockSpec((LANES, grp * dk), lambda b, i, t: (0, i)),
                  pl.BlockSpec((1, grp * dk), lambda b, i, t: (0, i)),
                  pl.BlockSpec((grp, 1, dv), lambda b, i, t: (i, 0, 0))],
        out_specs=pl.BlockSpec((1, sb, grp * dv), lambda b, i, t: (b, t, i)),
        out_shape=jax.ShapeDtypeStruct((batch, seq, h * dv), BF16),
        scratch_shapes=[pltpu.VMEM((grp, dv, dk), F32)],
        compiler_params=_cparams(("parallel", "parallel", "arbitrary")), name="gla_core",
    )(p3, p3, p3, p3, gl3, wg, gate_bias.reshape(1, h * dk), norm_gain.reshape(h, 1, dv))


def _causal_conv(x, w):
    width = w.shape[0]
    xp = jnp.concatenate([jnp.zeros((SUBLANES, x.shape[1]), x.dtype), x], axis=0)
    out = x * w[width - 1:width, :]
    for sh in range(1, width):
        out = out + pltpu.roll(xp, sh, axis=0)[SUBLANES:] * w[width - 1 - sh:width - sh, :]
    return out


def _l2norm(x):
    return x * lax.rsqrt(jnp.sum(x * x, axis=-1, keepdims=True) + NORM_EPS)


def _gdn_gate_kernel(bl_ref, al_ref, alog_ref, dt_ref, beta_ref, cum_ref):
    beta_ref[0] = jax.nn.sigmoid(bl_ref[0])
    log_alpha = -jnp.exp(alog_ref[...]) * _softplus(al_ref[0] + dt_ref[...])
    cum_ref[0] = _fdot(_tril_ones(CHUNK), log_alpha)


def _gdn_gates(gates, a_log, dt_bias, batch, seq):
    h = GDN_HEADS
    n = seq // CHUNK
    g4 = gates.reshape(batch, n, CHUNK, LANES)

    def to_pos_major(x):
        return x.transpose(0, 2, 3, 1).reshape(batch, CHUNK, h * n)

    spec = pl.BlockSpec((1, CHUNK, h * n), lambda b: (b, 0, 0))
    tab = pl.BlockSpec((1, h * n), lambda b: (0, 0))
    return pl.pallas_call(
        _gdn_gate_kernel, grid=(batch,),
        in_specs=[spec, spec, tab, tab], out_specs=[spec, spec],
        out_shape=[jax.ShapeDtypeStruct((batch, CHUNK, h * n), F32)] * 2,
        compiler_params=_cparams(("parallel",)), name="gdn_gates",
    )(to_pos_major(g4[..., :h]), to_pos_major(g4[..., h:2 * h]),
      jnp.repeat(a_log.astype(F32), n).reshape(1, h * n),
      jnp.repeat(dt_bias.astype(F32), n).reshape(1, h * n))


def _unit_lower_inverse_minus_eye(a, row, col):
    def blk(size):
        return (row // size) == (col // size)

    d = jnp.where(blk(8), a, 0.0)
    d2 = _bmm(d, d)
    d3 = _bmm(d2, d)
    d4 = _bmm(d2, d2)
    q = d2 - d - d3
    y = q + d4 + _bmm(q, d4)
    for size in (16, 32, 64):
        e = jnp.where(blk(size) & jnp.logical_not(blk(size // 2)), a, 0.0)
        t = e + _bmm(y, e)
        y = y - t - _bmm(t, y)
    return y


def _lane_column(x, lane_ids, idx):
    return jnp.sum(jnp.where(lane_ids == idx, x, 0.0), axis=1, keepdims=True)


def _gdn_kernel(q_ref, k_ref, v_ref, z_ref, wq_ref, wk_ref, wv_ref, betat_ref, cumt_ref, cumr_ref,
                gain_ref, o_ref, q_s, k_s, v_s, m_s, g_s, sb_s, st_s):
    seq = q_ref.shape[1]
    groups = st_s.shape[0]
    dk = st_s.shape[1]
    n_chunks = seq // CHUNK
    ub = min(GDN_SOLVE_BATCH, n_chunks)
    row = lax.broadcasted_iota(jnp.int32, (CHUNK, CHUNK), 0)
    col = lax.broadcasted_iota(jnp.int32, (CHUNK, CHUNK), 1)
    strict = row > col
    betat = betat_ref[0]
    cumt = cumt_ref[0]
    lane_ids = lax.broadcasted_iota(jnp.int32, cumt.shape, 1)
    dec_all = jnp.exp(cumt[CHUNK - 1:CHUNK, :])
    gain = gain_ref[...]

    for g in range(groups):
        cs = slice(g * dk, (g + 1) * dk)
        qg = _l2norm(_silu(_causal_conv(q_ref[0, :, cs].astype(F32), wq_ref[:, cs])))
        q_s[g] = (qg * dk ** -0.5).astype(q_s.dtype)
        k_s[g] = _l2norm(_silu(_causal_conv(k_ref[0, :, cs].astype(F32), wk_ref[:, cs])))
        v_s[g] = _silu(_causal_conv(v_ref[0, :, cs].astype(F32), wv_ref[:, cs]))

    def solve(g, n0):
        r = pl.ds(pl.multiple_of(n0 * CHUNK, ub * CHUNK), ub * CHUNK)
        k = k_s[g, r, :].reshape(ub, CHUNK, dk)
        v = v_s[g, r, :].reshape(ub, CHUNK, dk)
        cum_c = jnp.stack([_lane_column(cumt, lane_ids, g * n_chunks + n0 + u) for u in range(ub)])
        beta = jnp.stack([_lane_column(betat, lane_ids, g * n_chunks + n0 + u) for u in range(ub)])
        cum_r = cumr_ref[0, g, pl.ds(n0, ub)]
        rel = jnp.where(strict, jnp.exp(jnp.where(strict, cum_c - cum_r, 0.0)), 0.0)
        a = beta * rel * _bmm_nt(k, k)
        y = _unit_lower_inverse_minus_eye(a, row, col)
        rhs = jnp.concatenate([beta * v, (beta * jnp.exp(cum_c)) * k], axis=-1)
        sol = rhs + _bmm(y, rhs)
        ke = k * jnp.exp(cum_c[:, CHUNK - 1:CHUNK, :] - cum_c)
        gm = _bmm_tn(ke, sol)
        rs = pl.ds(pl.multiple_of(n0 * dk, ub * dk), ub * dk)
        g_s[g, rs, :] = gm[..., :dk].reshape(ub * dk, dk)
        m_s[g, rs, :] = gm[..., dk:].reshape(ub * dk, dk).astype(m_s.dtype)

    def solve_body(i, carry):
        for g in range(groups):
            solve(g, i * ub)
        return carry

    lax.fori_loop(0, n_chunks // ub, solve_body, 0)

    st_s[...] = jnp.zeros_like(st_s)

    def state_body(n, carry):
        rs = pl.ds(pl.multiple_of(n * dk, dk), dk)
        for g in range(groups):
            dec = _lane_column(dec_all, lane_ids[:1], g * n_chunks + n)
            state = st_s[g]
            state = state * dec - _bdot(m_s[g, rs, :], state) + g_s[g, rs, :]
            st_s[g] = state
            sb_s[g, rs, :] = state.astype(sb_s.dtype)
        return carry

    lax.fori_loop(0, n_chunks, state_body, 0)

    def readout(g, n0):
        r = pl.ds(pl.multiple_of(n0 * CHUNK, ub * CHUNK), ub * CHUNK)
        rs = pl.ds(pl.multiple_of(n0 * dk, ub * dk), ub * dk)
        cs = slice(g * dk, (g + 1) * dk)
        o = _bmm(q_s[g, r, :].reshape(ub, CHUNK, dk), sb_s[g, rs, :].reshape(ub, dk, dk))
        o = _rms(o.reshape(ub * CHUNK, dk), gain)
        o_ref[0, r, cs] = (o * _silu(z_ref[0, r, cs].astype(F32))).astype(o_ref.dtype)

    def readout_body(i, carry):
        for g in range(groups):
            readout(g, i * ub)
        return carry

    lax.fori_loop(0, n_chunks // ub, readout_body, 0)


def _gdn_core(proj, gates, conv_w, a_log, dt_bias, norm_gain, batch, seq):
    h = GDN_HEADS
    grp = GDN_GROUP
    ng = h // grp
    dk = norm_gain.shape[-1]
    n = seq // CHUNK
    p3 = proj.reshape(batch, seq, -1)
    beta_t, cum_t = _gdn_gates(gates, a_log, dt_bias, batch, seq)
    cum_r = cum_t.reshape(batch, CHUNK, h, n).transpose(0, 2, 3, 1)[:, :, :, None, :]
    head = lambda off: pl.BlockSpec((1, seq, grp * dk), lambda b, i: (b, 0, off + i))
    convw = lambda off: pl.BlockSpec((CONV_WIDTH, grp * dk), lambda b, i: (0, off + i))
    gate_spec = pl.BlockSpec((1, CHUNK, grp * n), lambda b, i: (b, 0, i))
    per_head = lambda dt: pltpu.VMEM((grp, seq, dk), dt)
    per_chunk = lambda dt: pltpu.VMEM((grp, n * dk, dk), dt)
    return pl.pallas_call(
        _gdn_kernel, grid=(batch, ng),
        in_specs=[head(0), head(ng), head(2 * ng), head(3 * ng),
                  convw(0), convw(ng), convw(2 * ng),
                  gate_spec, gate_spec,
                  pl.BlockSpec((1, grp, n, 1, CHUNK), lambda b, i: (b, i, 0, 0, 0)),
                  pl.BlockSpec((1, dk), lambda b, i: (0, 0))],
        out_specs=pl.BlockSpec((1, seq, grp * dk), lambda b, i: (b, 0, i)),
        out_shape=jax.ShapeDtypeStruct((batch, seq, h * dk), BF16),
        scratch_shapes=[per_head(BF16), per_head(F32), per_head(F32), per_chunk(BF16), per_chunk(F32),
                        per_chunk(BF16), pltpu.VMEM((grp, dk, dk), F32)],
        compiler_params=_cparams(("parallel", "parallel")), name="gdn_core",
    )(p3, p3, p3, p3, conv_w, conv_w, conv_w, beta_t, cum_t, cum_r, norm_gain.reshape(1, dk))


def _lru_kernel(x_ref, y_ref, cw_ref, cb_ref, w_ref, br_ref, bi_ref, lam_ref, o_ref, a_s, u_s):
    seq = x_ref.shape[1]
    width = x_ref.shape[2]
    x = _causal_conv(x_ref[0].astype(F32), cw_ref[...]) + cb_ref[...]
    gates = _bdot(x, w_ref[0])
    r = jax.nn.sigmoid(gates[:, :width] + br_ref[...])
    i = jax.nn.sigmoid(gates[:, width:] + bi_ref[...])
    a = jnp.exp((-LRU_C * _softplus(-lam_ref[...])) * r)
    a_s[...] = a
    t = 1.0 - a * a
    u_s[...] = jnp.where(t > 0.0, t * lax.rsqrt(t), 0.0) * (i * x)
    row = lax.broadcasted_iota(jnp.int32, (SUBLANES, width), 0)

    def scan_block(t, h_prev):
        rows = pl.ds(pl.multiple_of(t * SUBLANES, SUBLANES), SUBLANES)
        a = a_s[rows, :]
        u = u_s[rows, :]
        for d in (1, 2, 4):
            m = row >= d
            u = jnp.where(m, a * pltpu.roll(u, d, axis=0) + u, u)
            a = jnp.where(m, a * pltpu.roll(a, d, axis=0), a)
        h = u + a * h_prev
        y = jax.nn.gelu(y_ref[0, rows, :].astype(F32))
        o_ref[0, rows, :] = (h * y).astype(o_ref.dtype)
        return jnp.broadcast_to(h[SUBLANES - 1:SUBLANES, :], h.shape)

    unroll = 8

    def body(j, h_prev):
        for s in range(unroll):
            h_prev = scan_block(j * unroll + s, h_prev)
        return h_prev

    lax.fori_loop(0, seq // (SUBLANES * unroll), body, jnp.zeros((SUBLANES, width), F32))


def _lru_core(proj, conv_w, conv_b, w_r, b_r, w_i, b_i, lam, batch, seq):
    width = conv_w.shape[-1]
    nb = width // LRU_BLOCK
    p3 = proj.reshape(batch, seq, -1)
    w_ri = jnp.concatenate([w_r, w_i], axis=-1).astype(BF16)
    vec = lambda: pl.BlockSpec((1, LRU_BLOCK), lambda b, j: (0, j))
    return pl.pallas_call(
        _lru_kernel, grid=(batch, nb),
        in_specs=[pl.BlockSpec((1, seq, LRU_BLOCK), lambda b, j: (b, 0, j)),
                  pl.BlockSpec((1, seq, LRU_BLOCK), lambda b, j: (b, 0, nb + j)),
                  pl.BlockSpec((CONV_WIDTH, LRU_BLOCK), lambda b, j: (0, j)),
                  vec(),
                  pl.BlockSpec((1, LRU_BLOCK, 2 * LRU_BLOCK), lambda b, j: (j, 0, 0)),
                  vec(), vec(), vec()],
        out_specs=pl.BlockSpec((1, seq, LRU_BLOCK), lambda b, j: (b, 0, j)),
        out_shape=jax.ShapeDtypeStruct((batch, seq, width), BF16),
        scratch_shapes=[pltpu.VMEM((seq, LRU_BLOCK), F32), pltpu.VMEM((seq, LRU_BLOCK), F32)],
        compiler_params=_cparams(("parallel", "parallel")), name="lru_core",
    )(p3, p3, conv_w, conv_b.reshape(1, width), w_ri, b_r.reshape(1, width), b_i.reshape(1, width),
      lam.reshape(1, width))


def kernel(x, norm1, norm2, final_norm, ret_w_in, ret_gn_gain, ret_w_out, gdn_w_in, gdn_conv_w, gdn_a_log, gdn_dt_bias, gdn_norm_gain, gdn_w_out, gla_w_in, gla_w_gate_up, gla_gate_bias, gla_norm_gain, gla_w_out, lru_w_in, lru_conv_w, lru_conv_b, lru_w_rgate, lru_b_rgate, lru_w_igate, lru_b_igate, lru_lambda, lru_w_out, mlp_w_up, mlp_w_down):
    batch, seq, d = x.shape
    depth = norm1.shape[0]
    xf = x.reshape(batch * seq, d)
    hn = _rmsnorm(xf, norm1[0], BF16)
    w_out_all = (ret_w_out, gdn_w_out, gla_w_out, lru_w_out)
    for layer in range(depth):
        m, j = layer % 4, layer // 4
        if m == 0:
            w_in = ret_w_in[j]
            if layer == 0:
                proj, wu, wd, wo = _in_proj(hn, w_in, w_in.shape[1], BF16, "ret_in",
                                            next_w=((mlp_w_up, 0), (mlp_w_down, 0), (ret_w_out, 0)))
            else:
                proj = _in_proj(hn, w_in, w_in.shape[1], BF16, "ret_in")
            o = _retention_core(proj, ret_gn_gain[j], batch, seq)
        elif m == 1:
            w_t = gdn_w_in[j].T
            n_main = w_t.shape[0] - 2 * GDN_HEADS
            proj = _in_proj(hn, w_t, n_main, BF16, "gdn_in", w_transposed=True)
            gates = _gate_proj(hn, w_t, n_main, 2 * GDN_HEADS, "gdn_gate_in")
            o = _gdn_core(proj, gates, gdn_conv_w[j], gdn_a_log[j], gdn_dt_bias[j], gdn_norm_gain[j], batch, seq)
        elif m == 2:
            w_t = gla_w_in[j].T
            n_main = w_t.shape[0] - GLA_GATE_RANK
            proj = _in_proj(hn, w_t, n_main, BF16, "gla_in", w_transposed=True)
            gate_low = _gate_proj(hn, w_t, n_main, GLA_GATE_RANK, "gla_gate_in")
            o = _gla_core(proj, gate_low, gla_w_gate_up[j], gla_gate_bias[j], gla_norm_gain[j], batch, seq)
        else:
            w_in = lru_w_in[j]
            proj = _in_proj(hn, w_in, w_in.shape[1], BF16, "lru_in")
            o = _lru_core(proj, lru_conv_w[j], lru_conv_b[j], lru_w_rgate[j], lru_b_rgate[j].reshape(-1),
                          lru_w_igate[j], lru_b_igate[j].reshape(-1), lru_lambda[j], batch, seq)
        xf, hn = _proj_res_norm(o.reshape(batch * seq, -1), wo, xf, norm2[layer])
        if layer + 1 < depth:
            nxt = layer + 1
            xf, hn, wu, wd, wo = _mlp(hn, wu, wd, xf, norm1[nxt], True, BF16,
                                      next_w=((mlp_w_up, nxt), (mlp_w_down, nxt), (w_out_all[nxt % 4], nxt // 4)))
        else:
            (out,) = _mlp(hn, wu, wd, xf, final_norm, False, x.dtype)
    return out.reshape(batch, seq, d)
```

```python
import functools

import jax
import jax.numpy as jnp
from jax import lax
from jax.experimental import pallas as pl
from jax.experimental.pallas import tpu as pltpu

F32 = jnp.float32
BF16 = jnp.bfloat16

CHUNK = 64
NORM_EPS = 1e-6
ROPE_BASE = 10000.0
RET_HEADS = 8
GDN_HEADS = 16
GLA_HEADS = 4
GLA_GATE_RANK = 16
GLA_TAU = 16.0
LRU_BLOCK = 128
LRU_C = 8.0
CONV_WIDTH = 4

LANES = 128
SUBLANES = 8
VMEM_LIMIT = 56 * 1024 * 1024

IN_PROJ_TILE = (1024, 1024)
OUT_PROJ_ROWS = 512
MLP_TILE = (512, 1024)
RET_GROUP = 4
GLA_GROUP = 4
RET_UNROLL = 4
GLA_UNROLL = 2
SEQ_BLOCK = 512
GDN_GROUP = 4
GDN_SOLVE_BATCH = 32


def _cparams(sem):
    return pltpu.CompilerParams(dimension_semantics=sem, vmem_limit_bytes=VMEM_LIMIT)


def _rms(x, g):
    return x * lax.rsqrt(jnp.mean(x * x, axis=-1, keepdims=True) + NORM_EPS) * g


def _softplus(x):
    return jnp.maximum(x, 0.0) + jnp.log1p(jnp.exp(-jnp.abs(x)))


def _silu(x):
    return x * jax.nn.sigmoid(x)


def _bdot(a, b):
    return jnp.dot(a.astype(BF16), b.astype(BF16), preferred_element_type=F32)


def _bdot_nt(a, b):
    return lax.dot_general(a.astype(BF16), b.astype(BF16), (((1,), (1,)), ((), ())),
                           preferred_element_type=F32)


def _bdot_tn(a, b):
    return lax.dot_general(a.astype(BF16), b.astype(BF16), (((0,), (0,)), ((), ())),
                           preferred_element_type=F32)


def _bmm(a, b):
    return jnp.einsum("uij,ujk->uik", a.astype(BF16), b.astype(BF16), preferred_element_type=F32)


def _bmm_nt(a, b):
    return jnp.einsum("uid,ujd->uij", a.astype(BF16), b.astype(BF16), preferred_element_type=F32)


def _bmm_tn(a, b):
    return jnp.einsum("ucd,uce->ude", a.astype(BF16), b.astype(BF16), preferred_element_type=F32)


def _fdot(a, b):
    return jnp.dot(a, b, precision=lax.Precision.HIGHEST, preferred_element_type=F32)


def _tril_ones(n):
    r = lax.broadcasted_iota(jnp.int32, (n, n), 0)
    c = lax.broadcasted_iota(jnp.int32, (n, n), 1)
    return (r >= c).astype(F32)


def _norm_kernel(x_ref, g_ref, o_ref):
    o_ref[...] = _rms(x_ref[...], g_ref[...]).astype(o_ref.dtype)


def _rmsnorm(x, g, out_dtype, tm=512):
    m, d = x.shape
    return pl.pallas_call(
        _norm_kernel, grid=(m // tm,),
        in_specs=[pl.BlockSpec((tm, d), lambda i: (i, 0)), pl.BlockSpec((1, d), lambda i: (0, 0))],
        out_specs=pl.BlockSpec((tm, d), lambda i: (i, 0)),
        out_shape=jax.ShapeDtypeStruct((m, d), out_dtype),
        compiler_params=_cparams(("parallel",)), name="rmsnorm")(x, g.reshape(1, d))


def _ride_along(next_w, n_steps, step_index):
    n_slabs = 1 << (n_steps.bit_length() - 1)
    slab_index = lambda *ids: jnp.minimum(step_index(*ids), n_slabs - 1)
    in_specs, out_specs, out_shape, args = [], [], [], []
    for w_all, layer in next_w:
        _, rows, cols = w_all.shape
        in_specs.append(pl.BlockSpec((None, rows // n_slabs, cols),
                                     lambda *ids, l=layer: (l, slab_index(*ids), 0)))
        out_specs.append(pl.BlockSpec((rows // n_slabs, cols), lambda *ids: (slab_index(*ids), 0)))
        out_shape.append(jax.ShapeDtypeStruct((rows, cols), BF16))
        args.append(w_all)
    return in_specs, out_specs, out_shape, args


def _in_proj_kernel(a_ref, w_ref, *refs, w_transposed, convert):
    f32_refs, o_ref, bf16_refs, wb_ref = refs[:convert], refs[convert], refs[convert + 1:-1], refs[-1]

    @pl.when(pl.program_id(1) == 0)
    def _():
        wb_ref[...] = w_ref[...].astype(wb_ref.dtype)

    dims = (((1,), (1,)), ((), ())) if w_transposed else (((1,), (0,)), ((), ()))
    o_ref[...] = lax.dot_general(a_ref[...], wb_ref[...], dims, preferred_element_type=F32).astype(o_ref.dtype)
    for src, dst in zip(f32_refs, bf16_refs):
        dst[...] = src[...].astype(dst.dtype)


def _in_proj(a, w, n_cols, out_dtype, name, w_transposed=False, next_w=()):
    m, k = a.shape
    tm, tn = IN_PROJ_TILE
    tm, tn = min(tm, m), min(tn, n_cols)
    nj, ni = n_cols // tn, m // tm
    if w_transposed:
        w_spec = pl.BlockSpec((tn, k), lambda j, i: (j, 0))
        w_scratch = pltpu.VMEM((tn, k), BF16)
    else:
        w_spec = pl.BlockSpec((k, tn), lambda j, i: (0, j))
        w_scratch = pltpu.VMEM((k, tn), BF16)
    ra_in, ra_out, ra_shape, ra_args = _ride_along(next_w, nj * ni, lambda j, i: j * ni + i)
    res = pl.pallas_call(
        functools.partial(_in_proj_kernel, w_transposed=w_transposed, convert=len(next_w)), grid=(nj, ni),
        in_specs=[pl.BlockSpec((tm, k), lambda j, i: (i, 0)), w_spec] + ra_in,
        out_specs=[pl.BlockSpec((tm, tn), lambda j, i: (i, j))] + ra_out,
        out_shape=[jax.ShapeDtypeStruct((m, n_cols), out_dtype)] + ra_shape,
        scratch_shapes=[w_scratch],
        compiler_params=_cparams(("arbitrary" if next_w else "parallel", "arbitrary")),
        name=name)(a, w, *ra_args)
    return res if next_w else res[0]


def _gate_proj_kernel(a_ref, w_ref, o_ref):
    n = w_ref.shape[0]
    res = lax.dot_general(a_ref[...], w_ref[...].astype(BF16), (((1,), (1,)), ((), ())),
                          preferred_element_type=F32)
    o_ref[...] = jnp.zeros_like(o_ref)
    o_ref[:, :n] = res


def _gate_proj(a, w_t, row0, n_rows, name, tm=1024):
    m, k = a.shape
    tm = min(tm, m)
    return pl.pallas_call(
        _gate_proj_kernel, grid=(m // tm,),
        in_specs=[pl.BlockSpec((tm, k), lambda i: (i, 0)),
                  pl.BlockSpec((n_rows, k), lambda i: (row0 // n_rows, 0))],
        out_specs=pl.BlockSpec((tm, LANES), lambda i: (i, 0)),
        out_shape=jax.ShapeDtypeStruct((m, LANES), F32),
        compiler_params=_cparams(("parallel",)), name=name)(a, w_t)


def _proj_res_norm_kernel(a_ref, w_ref, x_ref, g_ref, xo_ref, ho_ref):
    x = x_ref[...] + jnp.dot(a_ref[...], w_ref[...], preferred_element_type=F32)
    xo_ref[...] = x
    ho_ref[...] = _rms(x, g_ref[...]).astype(ho_ref.dtype)


def _proj_res_norm(a, w, x, g):
    m, k = a.shape
    d = w.shape[1]
    tm = min(OUT_PROJ_ROWS, m)
    return pl.pallas_call(
        _proj_res_norm_kernel, grid=(m // tm,),
        in_specs=[pl.BlockSpec((tm, k), lambda i: (i, 0)),
                  pl.BlockSpec((k, d), lambda i: (0, 0), pipeline_mode=pl.Buffered(1)),
                  pl.BlockSpec((tm, d), lambda i: (i, 0)),
                  pl.BlockSpec((1, d), lambda i: (0, 0))],
        out_specs=[pl.BlockSpec((tm, d), lambda i: (i, 0)), pl.BlockSpec((tm, d), lambda i: (i, 0))],
        out_shape=[jax.ShapeDtypeStruct((m, d), F32), jax.ShapeDtypeStruct((m, d), BF16)],
        compiler_params=_cparams(("parallel",)), name="proj_res_norm")(a, w, x, g.reshape(1, d))


def _mlp_kernel(*refs, write_x, convert):
    h_ref, wu_ref, wd_ref, x_ref, g_ref = refs[:5]
    refs = refs[5:]
    f32_refs, refs = refs[:convert], refs[convert:]
    if write_x:
        xo_ref = refs[0]
        refs = refs[1:]
    ho_ref = refs[0]
    refs = refs[1:]
    bf16_refs, refs = refs[:convert], refs[convert:]
    acc_ref = xo_ref if write_x else refs[0]
    s = pl.program_id(1)

    @pl.when(s == 0)
    def _():
        acc_ref[...] = x_ref[...]

    u = jnp.dot(h_ref[...], wu_ref[...], preferred_element_type=F32)
    u = jnp.square(jnp.maximum(u, 0.0)).astype(BF16)
    acc_ref[...] += jnp.dot(u, wd_ref[...], preferred_element_type=F32)

    @pl.when(s == pl.num_programs(1) - 1)
    def _():
        ho_ref[...] = _rms(acc_ref[...], g_ref[...]).astype(ho_ref.dtype)

    for src, dst in zip(f32_refs, bf16_refs):
        dst[...] = src[...].astype(dst.dtype)


def _mlp(h, wu, wd, x, g, write_x, h_dtype, next_w=()):
    m, d = h.shape
    ff = wu.shape[1]
    tm, tf = MLP_TILE
    tm, tf = min(tm, m), min(tf, ff)
    n_ff = ff // tf
    ff_block = lambda i, s: jnp.where(i % 2 == 0, s, n_ff - 1 - s)
    row = pl.BlockSpec((tm, d), lambda i, s: (i, 0))
    in_specs = [row,
                pl.BlockSpec((d, tf), lambda i, s: (0, ff_block(i, s))),
                pl.BlockSpec((tf, d), lambda i, s: (ff_block(i, s), 0)),
                row,
                pl.BlockSpec((1, d), lambda i, s: (0, 0))]
    args = [h, wu, wd, x, g.reshape(1, d)]
    out_specs = [row]
    out_shape = [jax.ShapeDtypeStruct((m, d), h_dtype)]
    if write_x:
        out_specs = [row, row]
        out_shape = [jax.ShapeDtypeStruct((m, d), F32)] + out_shape
    ra_in, ra_out, ra_shape, ra_args = _ride_along(next_w, (m // tm) * n_ff, lambda i, s: i * n_ff + s)
    in_specs, out_specs, out_shape, args = in_specs + ra_in, out_specs + ra_out, out_shape + ra_shape, args + ra_args
    return pl.pallas_call(
        functools.partial(_mlp_kernel, write_x=write_x, convert=len(next_w)),
        grid=(m // tm, n_ff),
        in_specs=in_specs, out_specs=out_specs, out_shape=out_shape,
        scratch_shapes=[] if write_x else [pltpu.VMEM((tm, d), F32)],
        compiler_params=_cparams(("parallel", "arbitrary")), name="mlp")(*args)


def _rotary(x, c, s):
    half = x.shape[-1] // 2
    x1, x2 = x[:, :half], x[:, half:]
    return jnp.concatenate([x1 * c - x2 * s, x1 * s + x2 * c], axis=-1)


def _ret_kernel(q_ref, k_ref, v_ref, g_ref, cos_ref, sin_ref, dmat_ref, qd_ref, kd_ref, cd_ref,
                gain_ref, o_ref, state_ref):
    seq = q_ref.shape[1]
    groups = state_ref.shape[0]
    dk = q_ref.shape[2] // groups
    dv = v_ref.shape[2] // groups

    @pl.when(pl.program_id(2) == 0)
    def _():
        state_ref[...] = jnp.zeros_like(state_ref)

    def body(n, carry):
        r = pl.ds(pl.multiple_of(n * CHUNK, CHUNK), CHUNK)
        c = cos_ref[r, :]
        s = sin_ref[r, :]
        for h in range(groups):
            kc = slice(h * dk, (h + 1) * dk)
            vc = slice(h * dv, (h + 1) * dv)
            q = _rotary(q_ref[0, r, kc].astype(F32), c, s)
            k = _rotary(k_ref[0, r, kc].astype(F32), c, s) * dk ** -0.5
            v = v_ref[0, r, vc]
            scores = _bdot_nt(q, k) * dmat_ref[h]
            state = state_ref[h]
            o = _bdot(scores, v) + _bdot(q * qd_ref[h], state)
            state_ref[h] = state * cd_ref[h] + _bdot_tn(k * kd_ref[h], v)
            o = o - jnp.mean(o, axis=-1, keepdims=True)
            o = _rms(o, gain_ref[h])
            o_ref[0, r, vc] = (o * _silu(g_ref[0, r, vc].astype(F32))).astype(o_ref.dtype)
        return carry

    lax.fori_loop(0, seq // CHUNK, body, 0, unroll=RET_UNROLL)


def _retention_core(proj, gn_gain, batch, seq):
    h = RET_HEADS
    grp = RET_GROUP
    ng = h // grp
    dv = gn_gain.shape[-1]
    dk = dv // 2
    p3 = proj.reshape(batch, seq, -1)
    inv = ROPE_BASE ** (-jnp.arange(0, dk, 2, dtype=F32) / dk)
    ang = jnp.arange(seq, dtype=F32)[:, None] * inv[None, :]
    cos, sin = jnp.cos(ang), jnp.sin(ang)
    log_gamma = jnp.log1p(-jnp.exp2(-5.0 - jnp.arange(h, dtype=F32)))
    pos = jnp.arange(CHUNK, dtype=F32)
    dist = jnp.abs(pos[:, None] - pos[None, :])
    dmat = jnp.exp(log_gamma[:, None, None] * dist)
    qd = jnp.exp(log_gamma[:, None] * (pos + 1.0))[:, :, None]
    kd = jnp.exp(log_gamma[:, None] * (CHUNK - 1.0 - pos))[:, :, None]
    cd = jnp.exp(log_gamma * CHUNK)[:, None, None]
    sb = min(SEQ_BLOCK, seq)
    per_head = lambda shape: pl.BlockSpec((grp,) + shape, lambda b, i, t: (i, 0, 0))
    return pl.pallas_call(
        _ret_kernel, grid=(batch, ng, seq // sb),
        in_specs=[pl.BlockSpec((1, sb, grp * dk), lambda b, i, t: (b, t, i)),
                  pl.BlockSpec((1, sb, grp * dk), lambda b, i, t: (b, t, ng + i)),
                  pl.BlockSpec((1, sb, grp * dv), lambda b, i, t: (b, t, ng + i)),
                  pl.BlockSpec((1, sb, grp * dv), lambda b, i, t: (b, t, 2 * ng + i)),
                  pl.BlockSpec((sb, dk // 2), lambda b, i, t: (t, 0)),
                  pl.BlockSpec((sb, dk // 2), lambda b, i, t: (t, 0)),
                  per_head((CHUNK, CHUNK)), per_head((CHUNK, 1)), per_head((CHUNK, 1)),
                  per_head((1, 1)), per_head((1, dv))],
        out_specs=pl.BlockSpec((1, sb, grp * dv), lambda b, i, t: (b, t, i)),
        out_shape=jax.ShapeDtypeStruct((batch, seq, h * dv), BF16),
        scratch_shapes=[pltpu.VMEM((grp, dk, dv), F32)],
        compiler_params=_cparams(("parallel", "parallel", "arbitrary")), name="retention_core",
    )(p3, p3, p3, p3, cos, sin, dmat, qd, kd, cd, gn_gain.reshape(h, 1, dv))


def _gla_kernel(q_ref, k_ref, v_ref, r_ref, gl_ref, wg_ref, gb_ref, gain_ref, o_ref, state_ref):
    seq = q_ref.shape[1]
    groups = state_ref.shape[0]
    dk = q_ref.shape[2] // groups
    dv = v_ref.shape[2] // groups

    @pl.when(pl.program_id(2) == 0)
    def _():
        state_ref[...] = jnp.zeros_like(state_ref)

    tri = _tril_ones(CHUNK)
    row = lax.broadcasted_iota(jnp.int32, (CHUNK, CHUNK), 0)
    col = lax.broadcasted_iota(jnp.int32, (CHUNK, CHUNK), 1)

    def body(n, carry):
        r = pl.ds(pl.multiple_of(n * CHUNK, CHUNK), CHUNK)
        logit = _fdot(gl_ref[0, r, :], wg_ref[...]) + gb_ref[...]
        log_alpha = -_softplus(-logit) / GLA_TAU
        cum_all = _fdot(tri, log_alpha)
        for h in range(groups):
            kc = slice(h * dk, (h + 1) * dk)
            vc = slice(h * dv, (h + 1) * dv)
            cum = cum_all[:, kc]
            ref = cum[CHUNK // 2 - 1:CHUNK // 2, :]
            last = cum[CHUNK - 1:CHUNK, :]
            fwd, bwd = jnp.exp(cum - ref), jnp.exp(ref - cum)
            q = q_ref[0, r, kc].astype(F32) * dk ** -0.5
            k = k_ref[0, r, kc].astype(F32)
            v = v_ref[0, r, vc]
            s_lo = _bdot_nt(q * fwd, k * bwd)
            s_up = _bdot_nt(q * bwd, k * fwd)
            scores = jnp.where(row >= col, s_lo, s_up)
            state = state_ref[h]
            o = _bdot(scores, v) + _bdot_nt(q * jnp.exp(cum), state)
            state_ref[h] = state * jnp.exp(last) + _bdot_tn(v, k * jnp.exp(last - cum))
            o = _rms(o, gain_ref[h])
            o_ref[0, r, vc] = (o * _silu(r_ref[0, r, vc].astype(F32))).astype(o_ref.dtype)
        return carry

    lax.fori_loop(0, seq // CHUNK, body, 0, unroll=GLA_UNROLL)


def _gla_core(proj, gate_low, w_gate_up, gate_bias, norm_gain, batch, seq):
    h = GLA_HEADS
    grp = GLA_GROUP
    ng = h // grp
    dv = norm_gain.shape[-1]
    dk = dv // 2
    p3 = proj.reshape(batch, seq, -1)
    gl3 = gate_low.reshape(batch, seq, LANES)
    wg = jnp.zeros((LANES, h * dk), F32).at[:GLA_GATE_RANK].set(w_gate_up)
    sb = min(SEQ_BLOCK, seq)
    return pl.pallas_call(
        _gla_kernel, grid=(batch, ng, seq // sb),
        in_specs=[pl.BlockSpec((1, sb, grp * dk), lambda b, i, t: (b, t, i)),
                  pl.BlockSpec((1, sb, grp * dk), lambda b, i, t: (b, t, ng + i)),
                  pl.BlockSpec((1, sb, grp * dv), lambda b, i, t: (b, t, ng + i)),
                  pl.BlockSpec((1, sb, grp * dv), lambda b, i, t: (b, t, 2 * ng + i)),
                  pl.BlockSpec((1, sb, LANES), lambda b, i, t: (b, t, 0)),
                  pl.BlockSpec((LANES, grp * dk), lambda b, i, t: (0, i)),
                  pl.BlockSpec((1, grp * dk), lambda b, i, t: (0, i)),
                  pl.BlockSpec((grp, 1, dv), lambda b, i, t: (i, 0, 0))],
        out_specs=pl.BlockSpec((1, sb, grp * dv), lambda b, i, t: (b, t, i)),
        out_shape=jax.ShapeDtypeStruct((batch, seq, h * dv), BF16),
        scratch_shapes=[pltpu.VMEM((grp, dv, dk), F32)],
        compiler_params=_cparams(("parallel", "parallel", "arbitrary")), name="gla_core",
    )(p3, p3, p3, p3, gl3, wg, gate_bias.reshape(1, h * dk), norm_gain.reshape(h, 1, dv))


def _causal_conv(x, w):
    width = w.shape[0]
    xp = jnp.concatenate([jnp.zeros((SUBLANES, x.shape[1]), x.dtype), x], axis=0)
    out = x * w[width - 1:width, :]
    for sh in range(1, width):
        out = out + pltpu.roll(xp, sh, axis=0)[SUBLANES:] * w[width - 1 - sh:width - sh, :]
    return out


def _l2norm(x):
    return x * lax.rsqrt(jnp.sum(x * x, axis=-1, keepdims=True) + NORM_EPS)


def _gdn_gate_kernel(bl_ref, al_ref, alog_ref, dt_ref, beta_ref, cum_ref):
    beta_ref[0] = jax.nn.sigmoid(bl_ref[0])
    log_alpha = -jnp.exp(alog_ref[...]) * _softplus(al_ref[0] + dt_ref[...])
    cum_ref[0] = _fdot(_tril_ones(CHUNK), log_alpha)


def _gdn_gates(gates, a_log, dt_bias, batch, seq):
    h = GDN_HEADS
    n = seq // CHUNK
    g4 = gates.reshape(batch, n, CHUNK, LANES)

    def to_pos_major(x):
        return x.transpose(0, 2, 3, 1).reshape(batch, CHUNK, h * n)

    spec = pl.BlockSpec((1, CHUNK, h * n), lambda b: (b, 0, 0))
    tab = pl.BlockSpec((1, h * n), lambda b: (0, 0))
    return pl.pallas_call(
        _gdn_gate_kernel, grid=(batch,),
        in_specs=[spec, spec, tab, tab], out_specs=[spec, spec],
        out_shape=[jax.ShapeDtypeStruct((batch, CHUNK, h * n), F32)] * 2,
        compiler_params=_cparams(("parallel",)), name="gdn_gates",
    )(to_pos_major(g4[..., :h]), to_pos_major(g4[..., h:2 * h]),
      jnp.repeat(a_log.astype(F32), n).reshape(1, h * n),
      jnp.repeat(dt_bias.astype(F32), n).reshape(1, h * n))


def _unit_lower_inverse_minus_eye(a, row, col):
    def blk(size):
        return (row // size) == (col // size)

    d = jnp.where(blk(8), a, 0.0)
    d2 = _bmm(d, d)
    d3 = _bmm(d2, d)
    d4 = _bmm(d2, d2)
    q = d2 - d - d3
    y = q + d4 + _bmm(q, d4)
    for size in (16, 32, 64):
        e = jnp.where(blk(size) & jnp.logical_not(blk(size // 2)), a, 0.0)
        t = e + _bmm(y, e)
        y = y - t - _bmm(t, y)
    return y


def _lane_column(x, lane_ids, idx):
    return jnp.sum(jnp.where(lane_ids == idx, x, 0.0), axis=1, keepdims=True)


def _gdn_kernel(q_ref, k_ref, v_ref, z_ref, wq_ref, wk_ref, wv_ref, betat_ref, cumt_ref, cumr_ref,
                gain_ref, o_ref, q_s, k_s, v_s, m_s, g_s, sb_s, st_s):
    seq = q_ref.shape[1]
    groups = st_s.shape[0]
    dk = st_s.shape[1]
    n_chunks = seq // CHUNK
    ub = min(GDN_SOLVE_BATCH, n_chunks)
    row = lax.broadcasted_iota(jnp.int32, (CHUNK, CHUNK), 0)
    col = lax.broadcasted_iota(jnp.int32, (CHUNK, CHUNK), 1)
    strict = row > col
    betat = betat_ref[0]
    cumt = cumt_ref[0]
    lane_ids = lax.broadcasted_iota(jnp.int32, cumt.shape, 1)
    dec_all = jnp.exp(cumt[CHUNK - 1:CHUNK, :])
    gain = gain_ref[...]

    for g in range(groups):
        cs = slice(g * dk, (g + 1) * dk)
        qg = _l2norm(_silu(_causal_conv(q_ref[0, :, cs].astype(F32), wq_ref[:, cs])))
        q_s[g] = (qg * dk ** -0.5).astype(q_s.dtype)
        k_s[g] = _l2norm(_silu(_causal_conv(k_ref[0, :, cs].astype(F32), wk_ref[:, cs])))
        v_s[g] = _silu(_causal_conv(v_ref[0, :, cs].astype(F32), wv_ref[:, cs]))

    def solve(g, n0):
        r = pl.ds(pl.multiple_of(n0 * CHUNK, ub * CHUNK), ub * CHUNK)
        k = k_s[g, r, :].reshape(ub, CHUNK, dk)
        v = v_s[g, r, :].reshape(ub, CHUNK, dk)
        cum_c = jnp.stack([_lane_column(cumt, lane_ids, g * n_chunks + n0 + u) for u in range(ub)])
        beta = jnp.stack([_lane_column(betat, lane_ids, g * n_chunks + n0 + u) for u in range(ub)])
        cum_r = cumr_ref[0, g, pl.ds(n0, ub)]
        rel = jnp.where(strict, jnp.exp(jnp.where(strict, cum_c - cum_r, 0.0)), 0.0)
        a = beta * rel * _bmm_nt(k, k)
        y = _unit_lower_inverse_minus_eye(a, row, col)
        rhs = jnp.concatenate([beta * v, (beta * jnp.exp(cum_c)) * k], axis=-1)
        sol = rhs + _bmm(y, rhs)
        ke = k * jnp.exp(cum_c[:, CHUNK - 1:CHUNK, :] - cum_c)
        gm = _bmm_tn(ke, sol)
        rs = pl.ds(pl.multiple_of(n0 * dk, ub * dk), ub * dk)
        g_s[g, rs, :] = gm[..., :dk].reshape(ub * dk, dk)
        m_s[g, rs, :] = gm[..., dk:].reshape(ub * dk, dk).astype(m_s.dtype)

    def solve_body(i, carry):
        for g in range(groups):
            solve(g, i * ub)
        return carry

    lax.fori_loop(0, n_chunks // ub, solve_body, 0)

    st_s[...] = jnp.zeros_like(st_s)

    def state_body(n, carry):
        rs = pl.ds(pl.multiple_of(n * dk, dk), dk)
        for g in range(groups):
            dec = _lane_column(dec_all, lane_ids[:1], g * n_chunks + n)
            state = st_s[g]
            state = state * dec - _bdot(m_s[g, rs, :], state) + g_s[g, rs, :]
            st_s[g] = state
            sb_s[g, rs, :] = state.astype(sb_s.dtype)
        return carry

    lax.fori_loop(0, n_chunks, state_body, 0)

    def readout(g, n0):
        r = pl.ds(pl.multiple_of(n0 * CHUNK, ub * CHUNK), ub * CHUNK)
        rs = pl.ds(pl.multiple_of(n0 * dk, ub * dk), ub * dk)
        cs = slice(g * dk, (g + 1) * dk)
        o = _bmm(q_s[g, r, :].reshape(ub, CHUNK, dk), sb_s[g, rs, :].reshape(ub, dk, dk))
        o = _rms(o.reshape(ub * CHUNK, dk), gain)
        o_ref[0, r, cs] = (o * _silu(z_ref[0, r, cs].astype(F32))).astype(o_ref.dtype)

    def readout_body(i, carry):
        for g in range(groups):
            readout(g, i * ub)
        return carry

    lax.fori_loop(0, n_chunks // ub, readout_body, 0)


def _gdn_core(proj, gates, conv_w, a_log, dt_bias, norm_gain, batch, seq):
    h = GDN_HEADS
    grp = GDN_GROUP
    ng = h // grp
    dk = norm_gain.shape[-1]
    n = seq // CHUNK
    p3 = proj.reshape(batch, seq, -1)
    beta_t, cum_t = _gdn_gates(gates, a_log, dt_bias, batch, seq)
    cum_r = cum_t.reshape(batch, CHUNK, h, n).transpose(0, 2, 3, 1)[:, :, :, None, :]
    head = lambda off: pl.BlockSpec((1, seq, grp * dk), lambda b, i: (b, 0, off + i))
    convw = lambda off: pl.BlockSpec((CONV_WIDTH, grp * dk), lambda b, i: (0, off + i))
    gate_spec = pl.BlockSpec((1, CHUNK, grp * n), lambda b, i: (b, 0, i))
    per_head = lambda dt: pltpu.VMEM((grp, seq, dk), dt)
    per_chunk = lambda dt: pltpu.VMEM((grp, n * dk, dk), dt)
    return pl.pallas_call(
        _gdn_kernel, grid=(batch, ng),
        in_specs=[head(0), head(ng), head(2 * ng), head(3 * ng),
                  convw(0), convw(ng), convw(2 * ng),
                  gate_spec, gate_spec,
                  pl.BlockSpec((1, grp, n, 1, CHUNK), lambda b, i: (b, i, 0, 0, 0)),
                  pl.BlockSpec((1, dk), lambda b, i: (0, 0))],
        out_specs=pl.BlockSpec((1, seq, grp * dk), lambda b, i: (b, 0, i)),
        out_shape=jax.ShapeDtypeStruct((batch, seq, h * dk), BF16),
        scratch_shapes=[per_head(BF16), per_head(F32), per_head(F32), per_chunk(BF16), per_chunk(F32),
                        per_chunk(BF16), pltpu.VMEM((grp, dk, dk), F32)],
        compiler_params=_cparams(("parallel", "parallel")), name="gdn_core",
    )(p3, p3, p3, p3, conv_w, conv_w, conv_w, beta_t, cum_t, cum_r, norm_gain.reshape(1, dk))


def _lru_kernel(x_ref, y_ref, cw_ref, cb_ref, w_ref, br_ref, bi_ref, lam_ref, o_ref, a_s, u_s):
    seq = x_ref.shape[1]
    width = x_ref.shape[2]
    x = _causal_conv(x_ref[0].astype(F32), cw_ref[...]) + cb_ref[...]
    gates = _bdot(x, w_ref[0])
    r = jax.nn.sigmoid(gates[:, :width] + br_ref[...])
    i = jax.nn.sigmoid(gates[:, width:] + bi_ref[...])
    a = jnp.exp((-LRU_C * _softplus(-lam_ref[...])) * r)
    a_s[...] = a
    t = 1.0 - a * a
    u_s[...] = jnp.where(t > 0.0, t * lax.rsqrt(t), 0.0) * (i * x)
    row = lax.broadcasted_iota(jnp.int32, (SUBLANES, width), 0)

    def scan_block(t, h_prev):
        rows = pl.ds(pl.multiple_of(t * SUBLANES, SUBLANES), SUBLANES)
        a = a_s[rows, :]
        u = u_s[rows, :]
        for d in (1, 2, 4):
            m = row >= d
            u = jnp.where(m, a * pltpu.roll(u, d, axis=0) + u, u)
            a = jnp.where(m, a * pltpu.roll(a, d, axis=0), a)
        h = u + a * h_prev
        y = jax.nn.gelu(y_ref[0, rows, :].astype(F32))
        o_ref[0, rows, :] = (h * y).astype(o_ref.dtype)
        return jnp.broadcast_to(h[SUBLANES - 1:SUBLANES, :], h.shape)

    unroll = 8

    def body(j, h_prev):
        for s in range(unroll):
            h_prev = scan_block(j * unroll + s, h_prev)
        return h_prev

    lax.fori_loop(0, seq // (SUBLANES * unroll), body, jnp.zeros((SUBLANES, width), F32))


def _lru_core(proj, conv_w, conv_b, w_r, b_r, w_i, b_i, lam, batch, seq):
    width = conv_w.shape[-1]
    nb = width // LRU_BLOCK
    p3 = proj.reshape(batch, seq, -1)
    w_ri = jnp.concatenate([w_r, w_i], axis=-1).astype(BF16)
    vec = lambda: pl.BlockSpec((1, LRU_BLOCK), lambda b, j: (0, j))
    return pl.pallas_call(
        _lru_kernel, grid=(batch, nb),
        in_specs=[pl.BlockSpec((1, seq, LRU_BLOCK), lambda b, j: (b, 0, j)),
                  pl.BlockSpec((1, seq, LRU_BLOCK), lambda b, j: (b, 0, nb + j)),
                  pl.BlockSpec((CONV_WIDTH, LRU_BLOCK), lambda b, j: (0, j)),
                  vec(),
                  pl.BlockSpec((1, LRU_BLOCK, 2 * LRU_BLOCK), lambda b, j: (j, 0, 0)),
                  vec(), vec(), vec()],
        out_specs=pl.BlockSpec((1, seq, LRU_BLOCK), lambda b, j: (b, 0, j)),
        out_shape=jax.ShapeDtypeStruct((batch, seq, width), BF16),
        scratch_shapes=[pltpu.VMEM((seq, LRU_BLOCK), F32), pltpu.VMEM((seq, LRU_BLOCK), F32)],
        compiler_params=_cparams(("parallel", "parallel")), name="lru_core",
    )(p3, p3, conv_w, conv_b.reshape(1, width), w_ri, b_r.reshape(1, width), b_i.reshape(1, width),
      lam.reshape(1, width))


def kernel(x, norm1, norm2, final_norm, ret_w_in, ret_gn_gain, ret_w_out, gdn_w_in, gdn_conv_w, gdn_a_log, gdn_dt_bias, gdn_norm_gain, gdn_w_out, gla_w_in, gla_w_gate_up, gla_gate_bias, gla_norm_gain, gla_w_out, lru_w_in, lru_conv_w, lru_conv_b, lru_w_rgate, lru_b_rgate, lru_w_igate, lru_b_igate, lru_lambda, lru_w_out, mlp_w_up, mlp_w_down):
    batch, seq, d = x.shape
    depth = norm1.shape[0]
    xf = x.reshape(batch * seq, d)
    hn = _rmsnorm(xf, norm1[0], BF16)
    w_out_all = (ret_w_out, gdn_w_out, gla_w_out, lru_w_out)
    for layer in range(depth):
        m, j = layer % 4, layer // 4
        if m == 0:
            w_in = ret_w_in[j]
            if layer == 0:
                proj, wu, wd, wo = _in_proj(hn, w_in, w_in.shape[1], BF16, "ret_in",
                                            next_w=((mlp_w_up, 0), (mlp_w_down, 0), (ret_w_out, 0)))
            else:
                proj = _in_proj(hn, w_in, w_in.shape[1], BF16, "ret_in")
            o = _retention_core(proj, ret_gn_gain[j], batch, seq)
        elif m == 1:
            w_t = gdn_w_in[j].T
            n_main = w_t.shape[0] - 2 * GDN_HEADS
            proj = _in_proj(hn, w_t, n_main, BF16, "gdn_in", w_transposed=True)
            gates = _gate_proj(hn, w_t, n_main, 2 * GDN_HEADS, "gdn_gate_in")
            o = _gdn_core(proj, gates, gdn_conv_w[j], gdn_a_log[j], gdn_dt_bias[j], gdn_norm_gain[j], batch, seq)
        elif m == 2:
            w_t = gla_w_in[j].T
            n_main = w_t.shape[0] - GLA_GATE_RANK
            proj = _in_proj(hn, w_t, n_main, BF16, "gla_in", w_transposed=True)
            gate_low = _gate_proj(hn, w_t, n_main, GLA_GATE_RANK, "gla_gate_in")
            o = _gla_core(proj, gate_low, gla_w_gate_up[j], gla_gate_bias[j], gla_norm_gain[j], batch, seq)
        else:
            w_in = lru_w_in[j]
            proj = _in_proj(hn, w_in, w_in.shape[1], BF16, "lru_in")
            o = _lru_core(proj, lru_conv_w[j], lru_conv_b[j], lru_w_rgate[j], lru_b_rgate[j].reshape(-1),
                          lru_w_igate[j], lru_b_igate[j].reshape(-1), lru_lambda[j], batch, seq)
        xf, hn = _proj_res_norm(o.reshape(batch * seq, -1), wo, xf, norm2[layer])
        if layer + 1 < depth:
            nxt = layer + 1
            xf, hn, wu, wd, wo = _mlp(hn, wu, wd, xf, norm1[nxt], True, BF16,
                                      next_w=((mlp_w_up, nxt), (mlp_w_down, nxt), (w_out_all[nxt % 4], nxt // 4)))
        else:
            (out,) = _mlp(hn, wu, wd, xf, final_norm, False, x.dtype)
    return out.reshape(batch, seq, d)
```

```python
import functools

import jax
import jax.numpy as jnp
from jax import lax
from jax.experimental import pallas as pl
from jax.experimental.pallas import tpu as pltpu

F32 = jnp.float32
BF16 = jnp.bfloat16

CHUNK = 64
NORM_EPS = 1e-6
ROPE_BASE = 10000.0
RET_HEADS = 8
GDN_HEADS = 16
GLA_HEADS = 4
GLA_GATE_RANK = 16
GLA_TAU = 16.0
LRU_BLOCK = 128
LRU_C = 8.0
CONV_WIDTH = 4

LANES = 128
SUBLANES = 8
VMEM_LIMIT = 56 * 1024 * 1024

IN_PROJ_TILE = (1024, 1024)
OUT_PROJ_ROWS = 512
MLP_TILE = (512, 1024)
RET_GROUP = 4
GLA_GROUP = 4
RET_UNROLL = 4
GLA_UNROLL = 4
SEQ_BLOCK = 512
GDN_GROUP = 4
GDN_SOLVE_BATCH = 32


def _cparams(sem):
    return pltpu.CompilerParams(dimension_semantics=sem, vmem_limit_bytes=VMEM_LIMIT)


def _rms(x, g):
    return x * lax.rsqrt(jnp.mean(x * x, axis=-1, keepdims=True) + NORM_EPS) * g


def _softplus(x):
    return jnp.maximum(x, 0.0) + jnp.log1p(jnp.exp(-jnp.abs(x)))


def _silu(x):
    return x * jax.nn.sigmoid(x)


def _bdot(a, b):
    return jnp.dot(a.astype(BF16), b.astype(BF16), preferred_element_type=F32)


def _bdot_nt(a, b):
    return lax.dot_general(a.astype(BF16), b.astype(BF16), (((1,), (1,)), ((), ())),
                           preferred_element_type=F32)


def _bdot_tn(a, b):
    return lax.dot_general(a.astype(BF16), b.astype(BF16), (((0,), (0,)), ((), ())),
                           preferred_element_type=F32)


def _bmm(a, b):
    return jnp.einsum("uij,ujk->uik", a.astype(BF16), b.astype(BF16), preferred_element_type=F32)


def _bmm_nt(a, b):
    return jnp.einsum("uid,ujd->uij", a.astype(BF16), b.astype(BF16), preferred_element_type=F32)


def _bmm_tn(a, b):
    return jnp.einsum("ucd,uce->ude", a.astype(BF16), b.astype(BF16), preferred_element_type=F32)


def _fdot(a, b):
    return jnp.dot(a, b, precision=lax.Precision.HIGHEST, preferred_element_type=F32)


def _tril_ones(n):
    r = lax.broadcasted_iota(jnp.int32, (n, n), 0)
    c = lax.broadcasted_iota(jnp.int32, (n, n), 1)
    return (r >= c).astype(F32)


def _norm_kernel(x_ref, g_ref, o_ref):
    o_ref[...] = _rms(x_ref[...], g_ref[...]).astype(o_ref.dtype)


def _rmsnorm(x, g, out_dtype, tm=512):
    m, d = x.shape
    return pl.pallas_call(
        _norm_kernel, grid=(m // tm,),
        in_specs=[pl.BlockSpec((tm, d), lambda i: (i, 0)), pl.BlockSpec((1, d), lambda i: (0, 0))],
        out_specs=pl.BlockSpec((tm, d), lambda i: (i, 0)),
        out_shape=jax.ShapeDtypeStruct((m, d), out_dtype),
        compiler_params=_cparams(("parallel",)), name="rmsnorm")(x, g.reshape(1, d))


def _ride_along(next_w, n_steps, step_index):
    n_slabs = 1 << (n_steps.bit_length() - 1)
    slab_index = lambda *ids: jnp.minimum(step_index(*ids), n_slabs - 1)
    in_specs, out_specs, out_shape, args = [], [], [], []
    for w_all, layer in next_w:
        _, rows, cols = w_all.shape
        in_specs.append(pl.BlockSpec((None, rows // n_slabs, cols),
                                     lambda *ids, l=layer: (l, slab_index(*ids), 0)))
        out_specs.append(pl.BlockSpec((rows // n_slabs, cols), lambda *ids: (slab_index(*ids), 0)))
        out_shape.append(jax.ShapeDtypeStruct((rows, cols), BF16))
        args.append(w_all)
    return in_specs, out_specs, out_shape, args


def _with_ride_along(kernel_fn, n_in, n_out, convert):
    def wrapped(*refs):
        ins, refs = refs[:n_in], refs[n_in:]
        f32_refs, refs = refs[:convert], refs[convert:]
        outs, refs = refs[:n_out], refs[n_out:]
        bf16_refs, scratch = refs[:convert], refs[convert:]
        kernel_fn(*ins, *outs, *scratch)
        for src, dst in zip(f32_refs, bf16_refs):
            dst[...] = src[...].astype(dst.dtype)
    return wrapped


def _in_proj_kernel(a_ref, w_ref, *refs, w_transposed, convert):
    f32_refs, o_ref, bf16_refs, wb_ref = refs[:convert], refs[convert], refs[convert + 1:-1], refs[-1]

    @pl.when(pl.program_id(1) == 0)
    def _():
        wb_ref[...] = w_ref[...].astype(wb_ref.dtype)

    dims = (((1,), (1,)), ((), ())) if w_transposed else (((1,), (0,)), ((), ()))
    o_ref[...] = lax.dot_general(a_ref[...], wb_ref[...], dims, preferred_element_type=F32).astype(o_ref.dtype)
    for src, dst in zip(f32_refs, bf16_refs):
        dst[...] = src[...].astype(dst.dtype)


def _in_proj(a, w, n_cols, out_dtype, name, w_transposed=False, next_w=()):
    m, k = a.shape
    tm, tn = IN_PROJ_TILE
    tm, tn = min(tm, m), min(tn, n_cols)
    nj, ni = n_cols // tn, m // tm
    if w_transposed:
        w_spec = pl.BlockSpec((tn, k), lambda j, i: (j, 0))
        w_scratch = pltpu.VMEM((tn, k), BF16)
    else:
        w_spec = pl.BlockSpec((k, tn), lambda j, i: (0, j))
        w_scratch = pltpu.VMEM((k, tn), BF16)
    ra_in, ra_out, ra_shape, ra_args = _ride_along(next_w, nj * ni, lambda j, i: j * ni + i)
    res = pl.pallas_call(
        functools.partial(_in_proj_kernel, w_transposed=w_transposed, convert=len(next_w)), grid=(nj, ni),
        in_specs=[pl.BlockSpec((tm, k), lambda j, i: (i, 0)), w_spec] + ra_in,
        out_specs=[pl.BlockSpec((tm, tn), lambda j, i: (i, j))] + ra_out,
        out_shape=[jax.ShapeDtypeStruct((m, n_cols), out_dtype)] + ra_shape,
        scratch_shapes=[w_scratch],
        compiler_params=_cparams(("arbitrary" if next_w else "parallel", "arbitrary")),
        name=name)(a, w, *ra_args)
    return res if next_w else res[0]


def _gate_proj_kernel(a_ref, w_ref, o_ref):
    n = w_ref.shape[0]
    res = lax.dot_general(a_ref[...], w_ref[...].astype(BF16), (((1,), (1,)), ((), ())),
                          preferred_element_type=F32)
    o_ref[...] = jnp.zeros_like(o_ref)
    o_ref[:, :n] = res


def _gate_proj(a, w_t, row0, n_rows, name, tm=1024):
    m, k = a.shape
    tm = min(tm, m)
    return pl.pallas_call(
        _gate_proj_kernel, grid=(m // tm,),
        in_specs=[pl.BlockSpec((tm, k), lambda i: (i, 0)),
                  pl.BlockSpec((n_rows, k), lambda i: (row0 // n_rows, 0))],
        out_specs=pl.BlockSpec((tm, LANES), lambda i: (i, 0)),
        out_shape=jax.ShapeDtypeStruct((m, LANES), F32),
        compiler_params=_cparams(("parallel",)), name=name)(a, w_t)


def _proj_res_norm_kernel(a_ref, w_ref, x_ref, g_ref, xo_ref, ho_ref):
    x = x_ref[...] + jnp.dot(a_ref[...], w_ref[...], preferred_element_type=F32)
    xo_ref[...] = x
    ho_ref[...] = _rms(x, g_ref[...]).astype(ho_ref.dtype)


def _proj_res_norm(a, w, x, g):
    m, k = a.shape
    d = w.shape[1]
    tm = min(OUT_PROJ_ROWS, m)
    return pl.pallas_call(
        _proj_res_norm_kernel, grid=(m // tm,),
        in_specs=[pl.BlockSpec((tm, k), lambda i: (i, 0)),
                  pl.BlockSpec((k, d), lambda i: (0, 0), pipeline_mode=pl.Buffered(1)),
                  pl.BlockSpec((tm, d), lambda i: (i, 0)),
                  pl.BlockSpec((1, d), lambda i: (0, 0))],
        out_specs=[pl.BlockSpec((tm, d), lambda i: (i, 0)), pl.BlockSpec((tm, d), lambda i: (i, 0))],
        out_shape=[jax.ShapeDtypeStruct((m, d), F32), jax.ShapeDtypeStruct((m, d), BF16)],
        compiler_params=_cparams(("parallel",)), name="proj_res_norm")(a, w, x, g.reshape(1, d))


def _mlp_kernel(*refs, write_x, convert):
    h_ref, wu_ref, wd_ref, x_ref, g_ref = refs[:5]
    refs = refs[5:]
    f32_refs, refs = refs[:convert], refs[convert:]
    if write_x:
        xo_ref = refs[0]
        refs = refs[1:]
    ho_ref = refs[0]
    refs = refs[1:]
    bf16_refs, refs = refs[:convert], refs[convert:]
    acc_ref = xo_ref if write_x else refs[0]
    s = pl.program_id(1)

    @pl.when(s == 0)
    def _():
        acc_ref[...] = x_ref[...]

    u = jnp.dot(h_ref[...], wu_ref[...], preferred_element_type=F32)
    u = jnp.square(jnp.maximum(u, 0.0)).astype(BF16)
    acc_ref[...] += jnp.dot(u, wd_ref[...], preferred_element_type=F32)

    @pl.when(s == pl.num_programs(1) - 1)
    def _():
        ho_ref[...] = _rms(acc_ref[...], g_ref[...]).astype(ho_ref.dtype)

    for src, dst in zip(f32_refs, bf16_refs):
        dst[...] = src[...].astype(dst.dtype)


def _mlp(h, wu, wd, x, g, write_x, h_dtype, next_w=()):
    m, d = h.shape
    ff = wu.shape[1]
    tm, tf = MLP_TILE
    tm, tf = min(tm, m), min(tf, ff)
    n_ff = ff // tf
    ff_block = lambda i, s: jnp.where(i % 2 == 0, s, n_ff - 1 - s)
    row = pl.BlockSpec((tm, d), lambda i, s: (i, 0))
    in_specs = [row,
                pl.BlockSpec((d, tf), lambda i, s: (0, ff_block(i, s))),
                pl.BlockSpec((tf, d), lambda i, s: (ff_block(i, s), 0)),
                row,
                pl.BlockSpec((1, d), lambda i, s: (0, 0))]
    args = [h, wu, wd, x, g.reshape(1, d)]
    out_specs = [row]
    out_shape = [jax.ShapeDtypeStruct((m, d), h_dtype)]
    if write_x:
        out_specs = [row, row]
        out_shape = [jax.ShapeDtypeStruct((m, d), F32)] + out_shape
    ra_in, ra_out, ra_shape, ra_args = _ride_along(next_w, (m // tm) * n_ff, lambda i, s: i * n_ff + s)
    in_specs, out_specs, out_shape, args = in_specs + ra_in, out_specs + ra_out, out_shape + ra_shape, args + ra_args
    return pl.pallas_call(
        functools.partial(_mlp_kernel, write_x=write_x, convert=len(next_w)),
        grid=(m // tm, n_ff),
        in_specs=in_specs, out_specs=out_specs, out_shape=out_shape,
        scratch_shapes=[] if write_x else [pltpu.VMEM((tm, d), F32)],
        compiler_params=_cparams(("parallel", "arbitrary")), name="mlp")(*args)


def _rotary(x, c, s):
    half = x.shape[-1] // 2
    x1, x2 = x[:, :half], x[:, half:]
    return jnp.concatenate([x1 * c - x2 * s, x1 * s + x2 * c], axis=-1)


def _ret_kernel(q_ref, k_ref, v_ref, g_ref, cos_ref, sin_ref, dmat_ref, qd_ref, kd_ref, cd_ref,
                gain_ref, o_ref, state_ref):
    seq = q_ref.shape[1]
    groups = state_ref.shape[0]
    dk = q_ref.shape[2] // groups
    dv = v_ref.shape[2] // groups

    @pl.when(pl.program_id(2) == 0)
    def _():
        state_ref[...] = jnp.zeros_like(state_ref)

    def body(n, carry):
        r = pl.ds(pl.multiple_of(n * CHUNK, CHUNK), CHUNK)
        c = cos_ref[r, :]
        s = sin_ref[r, :]
        for h in range(groups):
            kc = slice(h * dk, (h + 1) * dk)
            vc = slice(h * dv, (h + 1) * dv)
            q = _rotary(q_ref[0, r, kc].astype(F32), c, s)
            k = _rotary(k_ref[0, r, kc].astype(F32), c, s) * dk ** -0.5
            v = v_ref[0, r, vc]
            scores = _bdot_nt(q, k) * dmat_ref[h]
            state = state_ref[h]
            o = _bdot(scores, v) + _bdot(q * qd_ref[h], state)
            state_ref[h] = state * cd_ref[h] + _bdot_tn(k * kd_ref[h], v)
            o = o - jnp.mean(o, axis=-1, keepdims=True)
            o = _rms(o, gain_ref[h])
            o_ref[0, r, vc] = (o * _silu(g_ref[0, r, vc].astype(F32))).astype(o_ref.dtype)
        return carry

    lax.fori_loop(0, seq // CHUNK, body, 0, unroll=RET_UNROLL)


def _retention_core(proj, gn_gain, batch, seq, next_w=()):
    h = RET_HEADS
    grp = RET_GROUP
    ng = h // grp
    dv = gn_gain.shape[-1]
    dk = dv // 2
    p3 = proj.reshape(batch, seq, -1)
    inv = ROPE_BASE ** (-jnp.arange(0, dk, 2, dtype=F32) / dk)
    ang = jnp.arange(seq, dtype=F32)[:, None] * inv[None, :]
    cos, sin = jnp.cos(ang), jnp.sin(ang)
    log_gamma = jnp.log1p(-jnp.exp2(-5.0 - jnp.arange(h, dtype=F32)))
    pos = jnp.arange(CHUNK, dtype=F32)
    dist = jnp.abs(pos[:, None] - pos[None, :])
    dmat = jnp.exp(log_gamma[:, None, None] * dist)
    qd = jnp.exp(log_gamma[:, None] * (pos + 1.0))[:, :, None]
    kd = jnp.exp(log_gamma[:, None] * (CHUNK - 1.0 - pos))[:, :, None]
    cd = jnp.exp(log_gamma * CHUNK)[:, None, None]
    sb = min(SEQ_BLOCK, seq)
    nt = seq // sb
    per_head = lambda shape: pl.BlockSpec((grp,) + shape, lambda b, i, t: (i, 0, 0))
    ra_in, ra_out, ra_shape, ra_args = _ride_along(next_w, batch * ng * nt, lambda b, i, t: (b * ng + i) * nt + t)
    in_specs = [pl.BlockSpec((1, sb, grp * dk), lambda b, i, t: (b, t, i)),
                pl.BlockSpec((1, sb, grp * dk), lambda b, i, t: (b, t, ng + i)),
                pl.BlockSpec((1, sb, grp * dv), lambda b, i, t: (b, t, ng + i)),
                pl.BlockSpec((1, sb, grp * dv), lambda b, i, t: (b, t, 2 * ng + i)),
                pl.BlockSpec((sb, dk // 2), lambda b, i, t: (t, 0)),
                pl.BlockSpec((sb, dk // 2), lambda b, i, t: (t, 0)),
                per_head((CHUNK, CHUNK)), per_head((CHUNK, 1)), per_head((CHUNK, 1)),
                per_head((1, 1)), per_head((1, dv))]
    res = pl.pallas_call(
        _with_ride_along(_ret_kernel, len(in_specs), 1, len(next_w)), grid=(batch, ng, nt),
        in_specs=in_specs + ra_in,
        out_specs=[pl.BlockSpec((1, sb, grp * dv), lambda b, i, t: (b, t, i))] + ra_out,
        out_shape=[jax.ShapeDtypeStruct((batch, seq, h * dv), BF16)] + ra_shape,
        scratch_shapes=[pltpu.VMEM((grp, dk, dv), F32)],
        compiler_params=_cparams(("parallel", "parallel", "arbitrary")), name="retention_core",
    )(p3, p3, p3, p3, cos, sin, dmat, qd, kd, cd, gn_gain.reshape(h, 1, dv), *ra_args)
    return res if next_w else res[0]


def _gla_kernel(q_ref, k_ref, v_ref, r_ref, gl_ref, wg_ref, gb_ref, gain_ref, o_ref, state_ref):
    seq = q_ref.shape[1]
    groups = state_ref.shape[0]
    dk = q_ref.shape[2] // groups
    dv = v_ref.shape[2] // groups

    @pl.when(pl.program_id(2) == 0)
    def _():
        state_ref[...] = jnp.zeros_like(state_ref)

    tri = _tril_ones(CHUNK)
    row = lax.broadcasted_iota(jnp.int32, (CHUNK, CHUNK), 0)
    col = lax.broadcasted_iota(jnp.int32, (CHUNK, CHUNK), 1)

    def body(n, carry):
        r = pl.ds(pl.multiple_of(n * CHUNK, CHUNK), CHUNK)
        logit = _fdot(gl_ref[0, r, :], wg_ref[...]) + gb_ref[...]
        log_alpha = -_softplus(-logit) / GLA_TAU
        cum_all = _fdot(tri, log_alpha)
        for h in range(groups):
            kc = slice(h * dk, (h + 1) * dk)
            vc = slice(h * dv, (h + 1) * dv)
            cum = cum_all[:, kc]
            ref = cum[CHUNK // 2 - 1:CHUNK // 2, :]
            last = cum[CHUNK - 1:CHUNK, :]
            fwd, bwd = jnp.exp(cum - ref), jnp.exp(ref - cum)
            q = q_ref[0, r, kc].astype(F32) * dk ** -0.5
            k = k_ref[0, r, kc].astype(F32)
            v = v_ref[0, r, vc]
            s_lo = _bdot_nt(q * fwd, k * bwd)
            s_up = _bdot_nt(q * bwd, k * fwd)
            scores = jnp.where(row >= col, s_lo, s_up)
            state = state_ref[h]
            o = _bdot(scores, v) + _bdot_nt(q * jnp.exp(cum), state)
            state_ref[h] = state * jnp.exp(last) + _bdot_tn(v, k * jnp.exp(last - cum))
            o = _rms(o, gain_ref[h])
            o_ref[0, r, vc] = (o * _silu(r_ref[0, r, vc].astype(F32))).astype(o_ref.dtype)
        return carry

    lax.fori_loop(0, seq // CHUNK, body, 0, unroll=GLA_UNROLL)


def _gla_core(proj, gate_low, w_gate_up, gate_bias, norm_gain, batch, seq, next_w=()):
    h = GLA_HEADS
    grp = GLA_GROUP
    ng = h // grp
    dv = norm_gain.shape[-1]
    dk = dv // 2
    p3 = proj.reshape(batch, seq, -1)
    gl3 = gate_low.reshape(batch, seq, LANES)
    wg = jnp.zeros((LANES, h * dk), F32).at[:GLA_GATE_RANK].set(w_gate_up)
    sb = min(SEQ_BLOCK, seq)
    nt = seq // sb
    ra_in, ra_out, ra_shape, ra_args = _ride_along(next_w, batch * ng * nt, lambda b, i, t: (b * ng + i) * nt + t)
    in_specs = [pl.BlockSpec((1, sb, grp * dk), lambda b, i, t: (b, t, i)),
                pl.BlockSpec((1, sb, grp * dk), lambda b, i, t: (b, t, ng + i)),
                pl.BlockSpec((1, sb, grp * dv), lambda b, i, t: (b, t, ng + i)),
                pl.BlockSpec((1, sb, grp * dv), lambda b, i, t: (b, t, 2 * ng + i)),
                pl.BlockSpec((1, sb, LANES), lambda b, i, t: (b, t, 0)),
                pl.BlockSpec((LANES, grp * dk), lambda b, i, t: (0, i)),
                pl.BlockSpec((1, grp * dk), lambda b, i, t: (0, i)),
                pl.BlockSpec((grp, 1, dv), lambda b, i, t: (i, 0, 0))]
    res = pl.pallas_call(
        _with_ride_along(_gla_kernel, len(in_specs), 1, len(next_w)), grid=(batch, ng, nt),
        in_specs=in_specs + ra_in,
        out_specs=[pl.BlockSpec((1, sb, grp * dv), lambda b, i, t: (b, t, i))] + ra_out,
        out_shape=[jax.ShapeDtypeStruct((batch, seq, h * dv), BF16)] + ra_shape,
        scratch_shapes=[pltpu.VMEM((grp, dv, dk), F32)],
        compiler_params=_cparams(("parallel", "parallel", "arbitrary")), name="gla_core",
    )(p3, p3, p3, p3, gl3, wg, gate_bias.reshape(1, h * dk), norm_gain.reshape(h, 1, dv), *ra_args)
    return res if next_w else res[0]


def _causal_conv(x, w):
    width = w.shape[0]
    xp = jnp.concatenate([jnp.zeros((SUBLANES, x.shape[1]), x.dtype), x], axis=0)
    out = x * w[width - 1:width, :]
    for sh in range(1, width):
        out = out + pltpu.roll(xp, sh, axis=0)[SUBLANES:] * w[width - 1 - sh:width - sh, :]
    return out


def _l2norm(x):
    return x * lax.rsqrt(jnp.sum(x * x, axis=-1, keepdims=True) + NORM_EPS)


def _gdn_gate_kernel(bl_ref, al_ref, alog_ref, dt_ref, beta_ref, cum_ref):
    beta_ref[0] = jax.nn.sigmoid(bl_ref[0])
    log_alpha = -jnp.exp(alog_ref[...]) * _softplus(al_ref[0] + dt_ref[...])
    cum_ref[0] = _fdot(_tril_ones(CHUNK), log_alpha)


def _gdn_gates(gates, a_log, dt_bias, batch, seq):
    h = GDN_HEADS
    n = seq // CHUNK
    g4 = gates.reshape(batch, n, CHUNK, LANES)

    def to_pos_major(x):
        return x.transpose(0, 2, 3, 1).reshape(batch, CHUNK, h * n)

    spec = pl.BlockSpec((1, CHUNK, h * n), lambda b: (b, 0, 0))
    tab = pl.BlockSpec((1, h * n), lambda b: (0, 0))
    return pl.pallas_call(
        _gdn_gate_kernel, grid=(batch,),
        in_specs=[spec, spec, tab, tab], out_specs=[spec, spec],
        out_shape=[jax.ShapeDtypeStruct((batch, CHUNK, h * n), F32)] * 2,
        compiler_params=_cparams(("parallel",)), name="gdn_gates",
    )(to_pos_major(g4[..., :h]), to_pos_major(g4[..., h:2 * h]),
      jnp.repeat(a_log.astype(F32), n).reshape(1, h * n),
      jnp.repeat(dt_bias.astype(F32), n).reshape(1, h * n))


def _unit_lower_inverse_minus_eye(a, row, col):
    def blk(size):
        return (row // size) == (col // size)

    d = jnp.where(blk(8), a, 0.0)
    d2 = _bmm(d, d)
    d3 = _bmm(d2, d)
    d4 = _bmm(d2, d2)
    q = d2 - d - d3
    y = q + d4 + _bmm(q, d4)
    for size in (16, 32, 64):
        e = jnp.where(blk(size) & jnp.logical_not(blk(size // 2)), a, 0.0)
        t = e + _bmm(y, e)
        y = y - t - _bmm(t, y)
    return y


def _lane_column(x, lane_ids, idx):
    return jnp.sum(jnp.where(lane_ids == idx, x, 0.0), axis=1, keepdims=True)


def _gdn_kernel(q_ref, k_ref, v_ref, z_ref, wq_ref, wk_ref, wv_ref, betat_ref, cumt_ref, cumr_ref,
                gain_ref, o_ref, q_s, k_s, v_s, m_s, g_s, sb_s, st_s):
    seq = q_ref.shape[1]
    groups = st_s.shape[0]
    dk = st_s.shape[1]
    n_chunks = seq // CHUNK
    ub = min(GDN_SOLVE_BATCH, n_chunks)
    row = lax.broadcasted_iota(jnp.int32, (CHUNK, CHUNK), 0)
    col = lax.broadcasted_iota(jnp.int32, (CHUNK, CHUNK), 1)
    strict = row > col
    betat = betat_ref[0]
    cumt = cumt_ref[0]
    lane_ids = lax.broadcasted_iota(jnp.int32, cumt.shape, 1)
    dec_all = jnp.exp(cumt[CHUNK - 1:CHUNK, :])
    gain = gain_ref[...]

    for g in range(groups):
        cs = slice(g * dk, (g + 1) * dk)
        qg = _l2norm(_silu(_causal_conv(q_ref[0, :, cs].astype(F32), wq_ref[:, cs])))
        q_s[g] = (qg * dk ** -0.5).astype(q_s.dtype)
        k_s[g] = _l2norm(_silu(_causal_conv(k_ref[0, :, cs].astype(F32), wk_ref[:, cs])))
        v_s[g] = _silu(_causal_conv(v_ref[0, :, cs].astype(F32), wv_ref[:, cs]))

    def solve(g, n0):
        r = pl.ds(pl.multiple_of(n0 * CHUNK, ub * CHUNK), ub * CHUNK)
        k = k_s[g, r, :].reshape(ub, CHUNK, dk)
        v = v_s[g, r, :].reshape(ub, CHUNK, dk)
        cum_c = jnp.stack([_lane_column(cumt, lane_ids, g * n_chunks + n0 + u) for u in range(ub)])
        beta = jnp.stack([_lane_column(betat, lane_ids, g * n_chunks + n0 + u) for u in range(ub)])
        cum_r = cumr_ref[0, g, pl.ds(n0, ub)]
        rel = jnp.where(strict, jnp.exp(jnp.where(strict, cum_c - cum_r, 0.0)), 0.0)
        a = beta * rel * _bmm_nt(k, k)
        y = _unit_lower_inverse_minus_eye(a, row, col)
        rhs = jnp.concatenate([beta * v, (beta * jnp.exp(cum_c)) * k], axis=-1)
        sol = rhs + _bmm(y, rhs)
        ke = k * jnp.exp(cum_c[:, CHUNK - 1:CHUNK, :] - cum_c)
        gm = _bmm_tn(ke, sol)
        rs = pl.ds(pl.multiple_of(n0 * dk, ub * dk), ub * dk)
        g_s[g, rs, :] = gm[..., :dk].reshape(ub * dk, dk)
        m_s[g, rs, :] = gm[..., dk:].reshape(ub * dk, dk).astype(m_s.dtype)

    def solve_body(i, carry):
        for g in range(groups):
            solve(g, i * ub)
        return carry

    lax.fori_loop(0, n_chunks // ub, solve_body, 0)

    st_s[...] = jnp.zeros_like(st_s)

    def state_body(n, carry):
        rs = pl.ds(pl.multiple_of(n * dk, dk), dk)
        for g in range(groups):
            dec = _lane_column(dec_all, lane_ids[:1], g * n_chunks + n)
            state = st_s[g]
            state = state * dec - _bdot(m_s[g, rs, :], state) + g_s[g, rs, :]
            st_s[g] = state
            sb_s[g, rs, :] = state.astype(sb_s.dtype)
        return carry

    lax.fori_loop(0, n_chunks, state_body, 0)

    def readout(g, n0):
        r = pl.ds(pl.multiple_of(n0 * CHUNK, ub * CHUNK), ub * CHUNK)
        rs = pl.ds(pl.multiple_of(n0 * dk, ub * dk), ub * dk)
        cs = slice(g * dk, (g + 1) * dk)
        o = _bmm(q_s[g, r, :].reshape(ub, CHUNK, dk), sb_s[g, rs, :].reshape(ub, dk, dk))
        o = _rms(o.reshape(ub * CHUNK, dk), gain)
        o_ref[0, r, cs] = (o * _silu(z_ref[0, r, cs].astype(F32))).astype(o_ref.dtype)

    def readout_body(i, carry):
        for g in range(groups):
            readout(g, i * ub)
        return carry

    lax.fori_loop(0, n_chunks // ub, readout_body, 0)


def _gdn_core(proj, gates, conv_w, a_log, dt_bias, norm_gain, batch, seq):
    h = GDN_HEADS
    grp = GDN_GROUP
    ng = h // grp
    dk = norm_gain.shape[-1]
    n = seq // CHUNK
    p3 = proj.reshape(batch, seq, -1)
    beta_t, cum_t = _gdn_gates(gates, a_log, dt_bias, batch, seq)
    cum_r = cum_t.reshape(batch, CHUNK, h, n).transpose(0, 2, 3, 1)[:, :, :, None, :]
    head = lambda off: pl.BlockSpec((1, seq, grp * dk), lambda b, i: (b, 0, off + i))
    convw = lambda off: pl.BlockSpec((CONV_WIDTH, grp * dk), lambda b, i: (0, off + i))
    gate_spec = pl.BlockSpec((1, CHUNK, grp * n), lambda b, i: (b, 0, i))
    per_head = lambda dt: pltpu.VMEM((grp, seq, dk), dt)
    per_chunk = lambda dt: pltpu.VMEM((grp, n * dk, dk), dt)
    return pl.pallas_call(
        _gdn_kernel, grid=(batch, ng),
        in_specs=[head(0), head(ng), head(2 * ng), head(3 * ng),
                  convw(0), convw(ng), convw(2 * ng),
                  gate_spec, gate_spec,
                  pl.BlockSpec((1, grp, n, 1, CHUNK), lambda b, i: (b, i, 0, 0, 0)),
                  pl.BlockSpec((1, dk), lambda b, i: (0, 0))],
        out_specs=pl.BlockSpec((1, seq, grp * dk), lambda b, i: (b, 0, i)),
        out_shape=jax.ShapeDtypeStruct((batch, seq, h * dk), BF16),
        scratch_shapes=[per_head(BF16), per_head(F32), per_head(F32), per_chunk(BF16), per_chunk(F32),
                        per_chunk(BF16), pltpu.VMEM((grp, dk, dk), F32)],
        compiler_params=_cparams(("parallel", "parallel")), name="gdn_core",
    )(p3, p3, p3, p3, conv_w, conv_w, conv_w, beta_t, cum_t, cum_r, norm_gain.reshape(1, dk))


def _lru_kernel(x_ref, y_ref, cw_ref, cb_ref, w_ref, br_ref, bi_ref, lam_ref, o_ref, a_s, u_s):
    seq = x_ref.shape[1]
    width = x_ref.shape[2]
    x = _causal_conv(x_ref[0].astype(F32), cw_ref[...]) + cb_ref[...]
    gates = _bdot(x, w_ref[0])
    r = jax.nn.sigmoid(gates[:, :width] + br_ref[...])
    i = jax.nn.sigmoid(gates[:, width:] + bi_ref[...])
    a = jnp.exp((-LRU_C * _softplus(-lam_ref[...])) * r)
    a_s[...] = a
    t = 1.0 - a * a
    u_s[...] = jnp.where(t > 0.0, t * lax.rsqrt(t), 0.0) * (i * x)
    row = lax.broadcasted_iota(jnp.int32, (SUBLANES, width), 0)

    def scan_block(t, h_prev):
        rows = pl.ds(pl.multiple_of(t * SUBLANES, SUBLANES), SUBLANES)
        a = a_s[rows, :]
        u = u_s[rows, :]
        for d in (1, 2, 4):
            m = row >= d
            u = jnp.where(m, a * pltpu.roll(u, d, axis=0) + u, u)
            a = jnp.where(m, a * pltpu.roll(a, d, axis=0), a)
        h = u + a * h_prev
        y = jax.nn.gelu(y_ref[0, rows, :].astype(F32))
        o_ref[0, rows, :] = (h * y).astype(o_ref.dtype)
        return jnp.broadcast_to(h[SUBLANES - 1:SUBLANES, :], h.shape)

    unroll = 8

    def body(j, h_prev):
        for s in range(unroll):
            h_prev = scan_block(j * unroll + s, h_prev)
        return h_prev

    lax.fori_loop(0, seq // (SUBLANES * unroll), body, jnp.zeros((SUBLANES, width), F32))


def _lru_core(proj, conv_w, conv_b, w_r, b_r, w_i, b_i, lam, batch, seq, next_w=()):
    width = conv_w.shape[-1]
    nb = width // LRU_BLOCK
    p3 = proj.reshape(batch, seq, -1)
    w_ri = jnp.concatenate([w_r, w_i], axis=-1).astype(BF16)
    vec = lambda: pl.BlockSpec((1, LRU_BLOCK), lambda b, j: (0, j))
    ra_in, ra_out, ra_shape, ra_args = _ride_along(next_w, batch * nb, lambda b, j: b * nb + j)
    in_specs = [pl.BlockSpec((1, seq, LRU_BLOCK), lambda b, j: (b, 0, j)),
                pl.BlockSpec((1, seq, LRU_BLOCK), lambda b, j: (b, 0, nb + j)),
                pl.BlockSpec((CONV_WIDTH, LRU_BLOCK), lambda b, j: (0, j)),
                vec(),
                pl.BlockSpec((1, LRU_BLOCK, 2 * LRU_BLOCK), lambda b, j: (j, 0, 0)),
                vec(), vec(), vec()]
    res = pl.pallas_call(
        _with_ride_along(_lru_kernel, len(in_specs), 1, len(next_w)), grid=(batch, nb),
        in_specs=in_specs + ra_in,
        out_specs=[pl.BlockSpec((1, seq, LRU_BLOCK), lambda b, j: (b, 0, j))] + ra_out,
        out_shape=[jax.ShapeDtypeStruct((batch, seq, width), BF16)] + ra_shape,
        scratch_shapes=[pltpu.VMEM((seq, LRU_BLOCK), F32), pltpu.VMEM((seq, LRU_BLOCK), F32)],
        compiler_params=_cparams(("parallel", "parallel")), name="lru_core",
    )(p3, p3, conv_w, conv_b.reshape(1, width), w_ri, b_r.reshape(1, width), b_i.reshape(1, width),
      lam.reshape(1, width), *ra_args)
    return res if next_w else res[0]


def kernel(x, norm1, norm2, final_norm, ret_w_in, ret_gn_gain, ret_w_out, gdn_w_in, gdn_conv_w, gdn_a_log, gdn_dt_bias, gdn_norm_gain, gdn_w_out, gla_w_in, gla_w_gate_up, gla_gate_bias, gla_norm_gain, gla_w_out, lru_w_in, lru_conv_w, lru_conv_b, lru_w_rgate, lru_b_rgate, lru_w_igate, lru_b_igate, lru_lambda, lru_w_out, mlp_w_up, mlp_w_down):
    batch, seq, d = x.shape
    depth = norm1.shape[0]
    xf = x.reshape(batch * seq, d)
    hn = _rmsnorm(xf, norm1[0], BF16)
    w_out_all = (ret_w_out, gdn_w_out, gla_w_out, lru_w_out)
    layer_w = lambda l: ((mlp_w_up, l), (mlp_w_down, l), (w_out_all[l % 4], l // 4))
    for layer in range(depth):
        m, j = layer % 4, layer // 4
        if m == 0:
            w_in = ret_w_in[j]
            proj, wu, wd, wo = _in_proj(hn, w_in, w_in.shape[1], BF16, "ret_in", next_w=layer_w(layer))
            if layer + 1 < depth:
                o, wu_n, wd_n, wo_n = _retention_core(proj, ret_gn_gain[j], batch, seq, next_w=layer_w(layer + 1))
            else:
                o = _retention_core(proj, ret_gn_gain[j], batch, seq)
        elif m == 1:
            wu, wd, wo = wu_n, wd_n, wo_n
            w_t = gdn_w_in[j].T
            n_main = w_t.shape[0] - 2 * GDN_HEADS
            proj = _in_proj(hn, w_t, n_main, BF16, "gdn_in", w_transposed=True)
            gates = _gate_proj(hn, w_t, n_main, 2 * GDN_HEADS, "gdn_gate_in")
            o = _gdn_core(proj, gates, gdn_conv_w[j], gdn_a_log[j], gdn_dt_bias[j], gdn_norm_gain[j], batch, seq)
        elif m == 2:
            w_t = gla_w_in[j].T
            n_main = w_t.shape[0] - GLA_GATE_RANK
            proj = _in_proj(hn, w_t, n_main, BF16, "gla_in", w_transposed=True)
            gate_low = _gate_proj(hn, w_t, n_main, GLA_GATE_RANK, "gla_gate_in")
            o, wu, wd, wo = _gla_core(proj, gate_low, gla_w_gate_up[j], gla_gate_bias[j], gla_norm_gain[j],
                                      batch, seq, next_w=layer_w(layer))
        else:
            w_in = lru_w_in[j]
            proj = _in_proj(hn, w_in, w_in.shape[1], BF16, "lru_in")
            o, wu, wd, wo = _lru_core(proj, lru_conv_w[j], lru_conv_b[j], lru_w_rgate[j],
                                      lru_b_rgate[j].reshape(-1), lru_w_igate[j], lru_b_igate[j].reshape(-1),
                                      lru_lambda[j], batch, seq, next_w=layer_w(layer))
        xf, hn = _proj_res_norm(o.reshape(batch * seq, -1), wo, xf, norm2[layer])
        if layer + 1 < depth:
            xf, hn = _mlp(hn, wu, wd, xf, norm1[layer + 1], True, BF16)
        else:
            (out,) = _mlp(hn, wu, wd, xf, final_norm, False, x.dtype)
    return out.reshape(batch, seq, d)
```

```python
import functools

import jax
import jax.numpy as jnp
from jax import lax
from jax.experimental import pallas as pl
from jax.experimental.pallas import tpu as pltpu

F32 = jnp.float32
BF16 = jnp.bfloat16

CHUNK = 64
NORM_EPS = 1e-6
ROPE_BASE = 10000.0
RET_HEADS = 8
GDN_HEADS = 16
GLA_HEADS = 4
GLA_GATE_RANK = 16
GLA_TAU = 16.0
LRU_BLOCK = 128
LRU_C = 8.0
CONV_WIDTH = 4

LANES = 128
SUBLANES = 8
VMEM_LIMIT = 56 * 1024 * 1024

IN_PROJ_TILE = (1024, 1024)
OUT_PROJ_ROWS = 512
MLP_TILE = (512, 1024)
RET_GROUP = 4
GLA_GROUP = 4
RET_UNROLL = 4
GLA_UNROLL = 4
SEQ_BLOCK = 512
GDN_GROUP = 4
GDN_SOLVE_BATCH = 32


def _cparams(sem):
    return pltpu.CompilerParams(dimension_semantics=sem, vmem_limit_bytes=VMEM_LIMIT)


def _rms(x, g):
    return x * lax.rsqrt(jnp.mean(x * x, axis=-1, keepdims=True) + NORM_EPS) * g


def _softplus(x):
    return jnp.maximum(x, 0.0) + jnp.log1p(jnp.exp(-jnp.abs(x)))


def _silu(x):
    return x * jax.nn.sigmoid(x)


def _bdot(a, b):
    return jnp.dot(a.astype(BF16), b.astype(BF16), preferred_element_type=F32)


def _bdot_nt(a, b):
    return lax.dot_general(a.astype(BF16), b.astype(BF16), (((1,), (1,)), ((), ())),
                           preferred_element_type=F32)


def _bdot_tn(a, b):
    return lax.dot_general(a.astype(BF16), b.astype(BF16), (((0,), (0,)), ((), ())),
                           preferred_element_type=F32)


def _bmm(a, b):
    return jnp.einsum("uij,ujk->uik", a.astype(BF16), b.astype(BF16), preferred_element_type=F32)


def _bmm_nt(a, b):
    return jnp.einsum("uid,ujd->uij", a.astype(BF16), b.astype(BF16), preferred_element_type=F32)


def _bmm_tn(a, b):
    return jnp.einsum("ucd,uce->ude", a.astype(BF16), b.astype(BF16), preferred_element_type=F32)


def _fdot(a, b):
    return jnp.dot(a, b, precision=lax.Precision.HIGHEST, preferred_element_type=F32)


def _tril_ones(n):
    r = lax.broadcasted_iota(jnp.int32, (n, n), 0)
    c = lax.broadcasted_iota(jnp.int32, (n, n), 1)
    return (r >= c).astype(F32)


def _norm_kernel(x_ref, g_ref, o_ref):
    o_ref[...] = _rms(x_ref[...], g_ref[...]).astype(o_ref.dtype)


def _rmsnorm(x, g, out_dtype, tm=512):
    m, d = x.shape
    return pl.pallas_call(
        _norm_kernel, grid=(m // tm,),
        in_specs=[pl.BlockSpec((tm, d), lambda i: (i, 0)), pl.BlockSpec((1, d), lambda i: (0, 0))],
        out_specs=pl.BlockSpec((tm, d), lambda i: (i, 0)),
        out_shape=jax.ShapeDtypeStruct((m, d), out_dtype),
        compiler_params=_cparams(("parallel",)), name="rmsnorm")(x, g.reshape(1, d))


def _ride_along(next_w, n_steps, step_index):
    n_slabs = 1 << (n_steps.bit_length() - 1)
    slab_index = lambda *ids: jnp.minimum(step_index(*ids), n_slabs - 1)
    in_specs, out_specs, out_shape, args = [], [], [], []
    for w_all, layer in next_w:
        _, rows, cols = w_all.shape
        in_specs.append(pl.BlockSpec((None, rows // n_slabs, cols),
                                     lambda *ids, l=layer: (l, slab_index(*ids), 0)))
        out_specs.append(pl.BlockSpec((rows // n_slabs, cols), lambda *ids: (slab_index(*ids), 0)))
        out_shape.append(jax.ShapeDtypeStruct((rows, cols), BF16))
        args.append(w_all)
    return in_specs, out_specs, out_shape, args


def _with_ride_along(kernel_fn, n_in, n_out, convert):
    def wrapped(*refs):
        ins, refs = refs[:n_in], refs[n_in:]
        f32_refs, refs = refs[:convert], refs[convert:]
        outs, refs = refs[:n_out], refs[n_out:]
        bf16_refs, scratch = refs[:convert], refs[convert:]
        kernel_fn(*ins, *outs, *scratch)
        for src, dst in zip(f32_refs, bf16_refs):
            dst[...] = src[...].astype(dst.dtype)
    return wrapped


def _in_proj_kernel(a_ref, w_ref, *refs, w_transposed, convert):
    f32_refs, o_ref, bf16_refs, wb_ref = refs[:convert], refs[convert], refs[convert + 1:-1], refs[-1]

    @pl.when(pl.program_id(1) == 0)
    def _():
        wb_ref[...] = w_ref[...].astype(wb_ref.dtype)

    dims = (((1,), (1,)), ((), ())) if w_transposed else (((1,), (0,)), ((), ()))
    o_ref[...] = lax.dot_general(a_ref[...], wb_ref[...], dims, preferred_element_type=F32).astype(o_ref.dtype)
    for src, dst in zip(f32_refs, bf16_refs):
        dst[...] = src[...].astype(dst.dtype)


def _in_proj(a, w, n_cols, out_dtype, name, w_transposed=False, next_w=()):
    m, k = a.shape
    tm, tn = IN_PROJ_TILE
    tm, tn = min(tm, m), min(tn, n_cols)
    nj, ni = n_cols // tn, m // tm
    if w_transposed:
        w_spec = pl.BlockSpec((tn, k), lambda j, i: (j, 0))
        w_scratch = pltpu.VMEM((tn, k), BF16)
    else:
        w_spec = pl.BlockSpec((k, tn), lambda j, i: (0, j))
        w_scratch = pltpu.VMEM((k, tn), BF16)
    ra_in, ra_out, ra_shape, ra_args = _ride_along(next_w, nj * ni, lambda j, i: j * ni + i)
    res = pl.pallas_call(
        functools.partial(_in_proj_kernel, w_transposed=w_transposed, convert=len(next_w)), grid=(nj, ni),
        in_specs=[pl.BlockSpec((tm, k), lambda j, i: (i, 0)), w_spec] + ra_in,
        out_specs=[pl.BlockSpec((tm, tn), lambda j, i: (i, j))] + ra_out,
        out_shape=[jax.ShapeDtypeStruct((m, n_cols), out_dtype)] + ra_shape,
        scratch_shapes=[w_scratch],
        compiler_params=_cparams(("arbitrary" if next_w else "parallel", "arbitrary")),
        name=name)(a, w, *ra_args)
    return res if next_w else res[0]


def _gate_proj_kernel(a_ref, w_ref, o_ref):
    n = w_ref.shape[0]
    res = lax.dot_general(a_ref[...], w_ref[...].astype(BF16), (((1,), (1,)), ((), ())),
                          preferred_element_type=F32)
    o_ref[...] = jnp.zeros_like(o_ref)
    o_ref[:, :n] = res


def _gate_proj(a, w_t, row0, n_rows, name, tm=1024):
    m, k = a.shape
    tm = min(tm, m)
    return pl.pallas_call(
        _gate_proj_kernel, grid=(m // tm,),
        in_specs=[pl.BlockSpec((tm, k), lambda i: (i, 0)),
                  pl.BlockSpec((n_rows, k), lambda i: (row0 // n_rows, 0))],
        out_specs=pl.BlockSpec((tm, LANES), lambda i: (i, 0)),
        out_shape=jax.ShapeDtypeStruct((m, LANES), F32),
        compiler_params=_cparams(("parallel",)), name=name)(a, w_t)


def _proj_res_norm_kernel(a_ref, w_ref, x_ref, g_ref, xo_ref, ho_ref):
    x = x_ref[...] + jnp.dot(a_ref[...], w_ref[...], preferred_element_type=F32)
    xo_ref[...] = x
    ho_ref[...] = _rms(x, g_ref[...]).astype(ho_ref.dtype)


def _proj_res_norm(a, w, x, g):
    m, k = a.shape
    d = w.shape[1]
    tm = min(OUT_PROJ_ROWS, m)
    return pl.pallas_call(
        _proj_res_norm_kernel, grid=(m // tm,),
        in_specs=[pl.BlockSpec((tm, k), lambda i: (i, 0)),
                  pl.BlockSpec((k, d), lambda i: (0, 0), pipeline_mode=pl.Buffered(1)),
                  pl.BlockSpec((tm, d), lambda i: (i, 0)),
                  pl.BlockSpec((1, d), lambda i: (0, 0))],
        out_specs=[pl.BlockSpec((tm, d), lambda i: (i, 0)), pl.BlockSpec((tm, d), lambda i: (i, 0))],
        out_shape=[jax.ShapeDtypeStruct((m, d), F32), jax.ShapeDtypeStruct((m, d), BF16)],
        compiler_params=_cparams(("parallel",)), name="proj_res_norm")(a, w, x, g.reshape(1, d))


def _mlp_kernel(*refs, write_x, convert):
    h_ref, wu_ref, wd_ref, x_ref, g_ref = refs[:5]
    refs = refs[5:]
    f32_refs, refs = refs[:convert], refs[convert:]
    if write_x:
        xo_ref = refs[0]
        refs = refs[1:]
    ho_ref = refs[0]
    refs = refs[1:]
    bf16_refs, refs = refs[:convert], refs[convert:]
    acc_ref = xo_ref if write_x else refs[0]
    s = pl.program_id(1)

    @pl.when(s == 0)
    def _():
        acc_ref[...] = x_ref[...]

    u = jnp.dot(h_ref[...], wu_ref[...], preferred_element_type=F32)
    u = jnp.square(jnp.maximum(u, 0.0)).astype(BF16)
    acc_ref[...] += jnp.dot(u, wd_ref[...], preferred_element_type=F32)

    @pl.when(s == pl.num_programs(1) - 1)
    def _():
        ho_ref[...] = _rms(acc_ref[...], g_ref[...]).astype(ho_ref.dtype)

    for src, dst in zip(f32_refs, bf16_refs):
        dst[...] = src[...].astype(dst.dtype)


def _mlp(h, wu, wd, x, g, write_x, h_dtype, next_w=()):
    m, d = h.shape
    ff = wu.shape[1]
    tm, tf = MLP_TILE
    tm, tf = min(tm, m), min(tf, ff)
    n_ff = ff // tf
    ff_block = lambda i, s: jnp.where(i % 2 == 0, s, n_ff - 1 - s)
    row = pl.BlockSpec((tm, d), lambda i, s: (i, 0))
    in_specs = [row,
                pl.BlockSpec((d, tf), lambda i, s: (0, ff_block(i, s))),
                pl.BlockSpec((tf, d), lambda i, s: (ff_block(i, s), 0)),
                row,
                pl.BlockSpec((1, d), lambda i, s: (0, 0))]
    args = [h, wu, wd, x, g.reshape(1, d)]
    out_specs = [row]
    out_shape = [jax.ShapeDtypeStruct((m, d), h_dtype)]
    if write_x:
        out_specs = [row, row]
        out_shape = [jax.ShapeDtypeStruct((m, d), F32)] + out_shape
    ra_in, ra_out, ra_shape, ra_args = _ride_along(next_w, (m // tm) * n_ff, lambda i, s: i * n_ff + s)
    in_specs, out_specs, out_shape, args = in_specs + ra_in, out_specs + ra_out, out_shape + ra_shape, args + ra_args
    return pl.pallas_call(
        functools.partial(_mlp_kernel, write_x=write_x, convert=len(next_w)),
        grid=(m // tm, n_ff),
        in_specs=in_specs, out_specs=out_specs, out_shape=out_shape,
        scratch_shapes=[] if write_x else [pltpu.VMEM((tm, d), F32)],
        compiler_params=_cparams(("parallel", "arbitrary")), name="mlp")(*args)


def _rotary(x, c, s):
    half = x.shape[-1] // 2
    x1, x2 = x[:, :half], x[:, half:]
    return jnp.concatenate([x1 * c - x2 * s, x1 * s + x2 * c], axis=-1)


def _ret_kernel(q_ref, k_ref, v_ref, g_ref, cos_ref, sin_ref, dmat_ref, qd_ref, kd_ref, cd_ref,
                gain_ref, o_ref, state_ref):
    seq = q_ref.shape[1]
    groups = state_ref.shape[0]
    dk = q_ref.shape[2] // groups
    dv = v_ref.shape[2] // groups

    @pl.when(pl.program_id(2) == 0)
    def _():
        state_ref[...] = jnp.zeros_like(state_ref)

    def body(n, carry):
        r = pl.ds(pl.multiple_of(n * CHUNK, CHUNK), CHUNK)
        c = cos_ref[r, :]
        s = sin_ref[r, :]
        for h in range(groups):
            kc = slice(h * dk, (h + 1) * dk)
            vc = slice(h * dv, (h + 1) * dv)
            q = _rotary(q_ref[0, r, kc].astype(F32), c, s)
            k = _rotary(k_ref[0, r, kc].astype(F32), c, s) * dk ** -0.5
            v = v_ref[0, r, vc]
            scores = _bdot_nt(q, k) * dmat_ref[h]
            state = state_ref[h]
            o = _bdot(scores, v) + _bdot(q * qd_ref[h], state)
            state_ref[h] = state * cd_ref[h] + _bdot_tn(k * kd_ref[h], v)
            o = o - jnp.mean(o, axis=-1, keepdims=True)
            o = _rms(o, gain_ref[h])
            o_ref[0, r, vc] = (o * _silu(g_ref[0, r, vc].astype(F32))).astype(o_ref.dtype)
        return carry

    lax.fori_loop(0, seq // CHUNK, body, 0, unroll=RET_UNROLL)


def _retention_core(proj, gn_gain, batch, seq, next_w=()):
    h = RET_HEADS
    grp = RET_GROUP
    ng = h // grp
    dv = gn_gain.shape[-1]
    dk = dv // 2
    p3 = proj.reshape(batch, seq, -1)
    inv = ROPE_BASE ** (-jnp.arange(0, dk, 2, dtype=F32) / dk)
    ang = jnp.arange(seq, dtype=F32)[:, None] * inv[None, :]
    cos, sin = jnp.cos(ang), jnp.sin(ang)
    log_gamma = jnp.log1p(-jnp.exp2(-5.0 - jnp.arange(h, dtype=F32)))
    pos = jnp.arange(CHUNK, dtype=F32)
    dist = jnp.abs(pos[:, None] - pos[None, :])
    dmat = jnp.exp(log_gamma[:, None, None] * dist)
    qd = jnp.exp(log_gamma[:, None] * (pos + 1.0))[:, :, None]
    kd = jnp.exp(log_gamma[:, None] * (CHUNK - 1.0 - pos))[:, :, None]
    cd = jnp.exp(log_gamma * CHUNK)[:, None, None]
    sb = min(SEQ_BLOCK, seq)
    nt = seq // sb
    per_head = lambda shape: pl.BlockSpec((grp,) + shape, lambda b, i, t: (i, 0, 0))
    ra_in, ra_out, ra_shape, ra_args = _ride_along(next_w, batch * ng * nt, lambda b, i, t: (b * ng + i) * nt + t)
    in_specs = [pl.BlockSpec((1, sb, grp * dk), lambda b, i, t: (b, t, i)),
                pl.BlockSpec((1, sb, grp * dk), lambda b, i, t: (b, t, ng + i)),
                pl.BlockSpec((1, sb, grp * dv), lambda b, i, t: (b, t, ng + i)),
                pl.BlockSpec((1, sb, grp * dv), lambda b, i, t: (b, t, 2 * ng + i)),
                pl.BlockSpec((sb, dk // 2), lambda b, i, t: (t, 0)),
                pl.BlockSpec((sb, dk // 2), lambda b, i, t: (t, 0)),
                per_head((CHUNK, CHUNK)), per_head((CHUNK, 1)), per_head((CHUNK, 1)),
                per_head((1, 1)), per_head((1, dv))]
    res = pl.pallas_call(
        _with_ride_along(_ret_kernel, len(in_specs), 1, len(next_w)), grid=(batch, ng, nt),
        in_specs=in_specs + ra_in,
        out_specs=[pl.BlockSpec((1, sb, grp * dv), lambda b, i, t: (b, t, i))] + ra_out,
        out_shape=[jax.ShapeDtypeStruct((batch, seq, h * dv), BF16)] + ra_shape,
        scratch_shapes=[pltpu.VMEM((grp, dk, dv), F32)],
        compiler_params=_cparams(("parallel", "parallel", "arbitrary")), name="retention_core",
    )(p3, p3, p3, p3, cos, sin, dmat, qd, kd, cd, gn_gain.reshape(h, 1, dv), *ra_args)
    return res if next_w else res[0]


def _gla_kernel(q_ref, k_ref, v_ref, r_ref, gl_ref, wg_ref, gb_ref, gain_ref, o_ref, state_ref):
    seq = q_ref.shape[1]
    groups = state_ref.shape[0]
    dk = q_ref.shape[2] // groups
    dv = v_ref.shape[2] // groups

    @pl.when(pl.program_id(2) == 0)
    def _():
        state_ref[...] = jnp.zeros_like(state_ref)

    tri = _tril_ones(CHUNK)
    row = lax.broadcasted_iota(jnp.int32, (CHUNK, CHUNK), 0)
    col = lax.broadcasted_iota(jnp.int32, (CHUNK, CHUNK), 1)

    def body(n, carry):
        r = pl.ds(pl.multiple_of(n * CHUNK, CHUNK), CHUNK)
        logit = _fdot(gl_ref[0, r, :], wg_ref[...]) + gb_ref[...]
        log_alpha = -_softplus(-logit) / GLA_TAU
        cum_all = _fdot(tri, log_alpha)
        for h in range(groups):
            kc = slice(h * dk, (h + 1) * dk)
            vc = slice(h * dv, (h + 1) * dv)
            cum = cum_all[:, kc]
            ref = cum[CHUNK // 2 - 1:CHUNK // 2, :]
            last = cum[CHUNK - 1:CHUNK, :]
            fwd, bwd = jnp.exp(cum - ref), jnp.exp(ref - cum)
            q = q_ref[0, r, kc].astype(F32) * dk ** -0.5
            k = k_ref[0, r, kc].astype(F32)
            v = v_ref[0, r, vc]
            s_lo = _bdot_nt(q * fwd, k * bwd)
            s_up = _bdot_nt(q * bwd, k * fwd)
            scores = jnp.where(row >= col, s_lo, s_up)
            state = state_ref[h]
            o = _bdot(scores, v) + _bdot_nt(q * jnp.exp(cum), state)
            state_ref[h] = state * jnp.exp(last) + _bdot_tn(v, k * jnp.exp(last - cum))
            o = _rms(o, gain_ref[h])
            o_ref[0, r, vc] = (o * _silu(r_ref[0, r, vc].astype(F32))).astype(o_ref.dtype)
        return carry

    lax.fori_loop(0, seq // CHUNK, body, 0, unroll=GLA_UNROLL)


def _gla_core(proj, gate_low, w_gate_up, gate_bias, norm_gain, batch, seq, next_w=()):
    h = GLA_HEADS
    grp = GLA_GROUP
    ng = h // grp
    dv = norm_gain.shape[-1]
    dk = dv // 2
    p3 = proj.reshape(batch, seq, -1)
    gl3 = gate_low.reshape(batch, seq, LANES)
    wg = jnp.zeros((LANES, h * dk), F32).at[:GLA_GATE_RANK].set(w_gate_up)
    sb = min(SEQ_BLOCK, seq)
    nt = seq // sb
    ra_in, ra_out, ra_shape, ra_args = _ride_along(next_w, batch * ng * nt, lambda b, i, t: (b * ng + i) * nt + t)
    in_specs = [pl.BlockSpec((1, sb, grp * dk), lambda b, i, t: (b, t, i)),
                pl.BlockSpec((1, sb, grp * dk), lambda b, i, t: (b, t, ng + i)),
                pl.BlockSpec((1, sb, grp * dv), lambda b, i, t: (b, t, ng + i)),
                pl.BlockSpec((1, sb, grp * dv), lambda b, i, t: (b, t, 2 * ng + i)),
                pl.BlockSpec((1, sb, LANES), lambda b, i, t: (b, t, 0)),
                pl.BlockSpec((LANES, grp * dk), lambda b, i, t: (0, i)),
                pl.BlockSpec((1, grp * dk), lambda b, i, t: (0, i)),
                pl.BlockSpec((grp, 1, dv), lambda b, i, t: (i, 0, 0))]
    res = pl.pallas_call(
        _with_ride_along(_gla_kernel, len(in_specs), 1, len(next_w)), grid=(batch, ng, nt),
        in_specs=in_specs + ra_in,
        out_specs=[pl.BlockSpec((1, sb, grp * dv), lambda b, i, t: (b, t, i))] + ra_out,
        out_shape=[jax.ShapeDtypeStruct((batch, seq, h * dv), BF16)] + ra_shape,
        scratch_shapes=[pltpu.VMEM((grp, dv, dk), F32)],
        compiler_params=_cparams(("parallel", "parallel", "arbitrary")), name="gla_core",
    )(p3, p3, p3, p3, gl3, wg, gate_bias.reshape(1, h * dk), norm_gain.reshape(h, 1, dv), *ra_args)
    return res if next_w else res[0]


def _causal_conv(x, w):
    width = w.shape[0]
    xp = jnp.concatenate([jnp.zeros((SUBLANES, x.shape[1]), x.dtype), x], axis=0)
    out = x * w[width - 1:width, :]
    for sh in range(1, width):
        out = out + pltpu.roll(xp, sh, axis=0)[SUBLANES:] * w[width - 1 - sh:width - sh, :]
    return out


def _l2norm(x):
    return x * lax.rsqrt(jnp.sum(x * x, axis=-1, keepdims=True) + NORM_EPS)


def _gdn_gate_kernel(bl_ref, al_ref, alog_ref, dt_ref, beta_ref, cum_ref):
    beta_ref[0] = jax.nn.sigmoid(bl_ref[0])
    log_alpha = -jnp.exp(alog_ref[...]) * _softplus(al_ref[0] + dt_ref[...])
    cum_ref[0] = _fdot(_tril_ones(CHUNK), log_alpha)


def _gdn_gates(gates, a_log, dt_bias, batch, seq):
    h = GDN_HEADS
    n = seq // CHUNK
    g4 = gates.reshape(batch, n, CHUNK, LANES)

    def to_pos_major(x):
        return x.transpose(0, 2, 3, 1).reshape(batch, CHUNK, h * n)

    spec = pl.BlockSpec((1, CHUNK, h * n), lambda b: (b, 0, 0))
    tab = pl.BlockSpec((1, h * n), lambda b: (0, 0))
    return pl.pallas_call(
        _gdn_gate_kernel, grid=(batch,),
        in_specs=[spec, spec, tab, tab], out_specs=[spec, spec],
        out_shape=[jax.ShapeDtypeStruct((batch, CHUNK, h * n), F32)] * 2,
        compiler_params=_cparams(("parallel",)), name="gdn_gates",
    )(to_pos_major(g4[..., :h]), to_pos_major(g4[..., h:2 * h]),
      jnp.repeat(a_log.astype(F32), n).reshape(1, h * n),
      jnp.repeat(dt_bias.astype(F32), n).reshape(1, h * n))


def _unit_lower_inverse_minus_eye(a, row, col):
    def blk(size):
        return (row // size) == (col // size)

    d = jnp.where(blk(8), a, 0.0)
    d2 = _bmm(d, d)
    d3 = _bmm(d2, d)
    d4 = _bmm(d2, d2)
    q = d2 - d - d3
    y = q + d4 + _bmm(q, d4)
    for size in (16, 32, 64):
        e = jnp.where(blk(size) & jnp.logical_not(blk(size // 2)), a, 0.0)
        t = e + _bmm(y, e)
        y = y - t - _bmm(t, y)
    return y


def _lane_column(x, lane_ids, idx):
    return jnp.sum(jnp.where(lane_ids == idx, x, 0.0), axis=1, keepdims=True)


def _gdn_kernel(q_ref, k_ref, v_ref, z_ref, wq_ref, wk_ref, wv_ref, betat_ref, cumt_ref, cumr_ref,
                gain_ref, o_ref, q_s, k_s, v_s, m_s, g_s, sb_s, st_s):
    seq = q_ref.shape[1]
    groups = st_s.shape[0]
    dk = st_s.shape[1]
    n_chunks = seq // CHUNK
    ub = min(GDN_SOLVE_BATCH, n_chunks)
    row = lax.broadcasted_iota(jnp.int32, (CHUNK, CHUNK), 0)
    col = lax.broadcasted_iota(jnp.int32, (CHUNK, CHUNK), 1)
    strict = row > col
    betat = betat_ref[0]
    cumt = cumt_ref[0]
    lane_ids = lax.broadcasted_iota(jnp.int32, cumt.shape, 1)
    dec_all = jnp.exp(cumt[CHUNK - 1:CHUNK, :])
    gain = gain_ref[...]

    for g in range(groups):
        cs = slice(g * dk, (g + 1) * dk)
        qg = _l2norm(_silu(_causal_conv(q_ref[0, :, cs].astype(F32), wq_ref[:, cs])))
        q_s[g] = (qg * dk ** -0.5).astype(q_s.dtype)
        k_s[g] = _l2norm(_silu(_causal_conv(k_ref[0, :, cs].astype(F32), wk_ref[:, cs])))
        v_s[g] = _silu(_causal_conv(v_ref[0, :, cs].astype(F32), wv_ref[:, cs]))

    def solve(g, n0):
        r = pl.ds(pl.multiple_of(n0 * CHUNK, ub * CHUNK), ub * CHUNK)
        k = k_s[g, r, :].reshape(ub, CHUNK, dk)
        v = v_s[g, r, :].reshape(ub, CHUNK, dk)
        cum_c = jnp.stack([_lane_column(cumt, lane_ids, g * n_chunks + n0 + u) for u in range(ub)])
        beta = jnp.stack([_lane_column(betat, lane_ids, g * n_chunks + n0 + u) for u in range(ub)])
        cum_r = cumr_ref[0, g, pl.ds(n0, ub)]
        rel = jnp.where(strict, jnp.exp(jnp.where(strict, cum_c - cum_r, 0.0)), 0.0)
        a = beta * rel * _bmm_nt(k, k)
        y = _unit_lower_inverse_minus_eye(a, row, col)
        rhs = jnp.concatenate([beta * v, (beta * jnp.exp(cum_c)) * k], axis=-1)
        sol = rhs + _bmm(y, rhs)
        ke = k * jnp.exp(cum_c[:, CHUNK - 1:CHUNK, :] - cum_c)
        gm = _bmm_tn(ke, sol)
        rs = pl.ds(pl.multiple_of(n0 * dk, ub * dk), ub * dk)
        g_s[g, rs, :] = gm[..., :dk].reshape(ub * dk, dk)
        m_s[g, rs, :] = gm[..., dk:].reshape(ub * dk, dk).astype(m_s.dtype)

    def solve_body(i, carry):
        for g in range(groups):
            solve(g, i * ub)
        return carry

    lax.fori_loop(0, n_chunks // ub, solve_body, 0)

    st_s[...] = jnp.zeros_like(st_s)

    def state_body(n, carry):
        rs = pl.ds(pl.multiple_of(n * dk, dk), dk)
        for g in range(groups):
            dec = _lane_column(dec_all, lane_ids[:1], g * n_chunks + n)
            state = st_s[g]
            state = state * dec - _bdot(m_s[g, rs, :], state) + g_s[g, rs, :]
            st_s[g] = state
            sb_s[g, rs, :] = state.astype(sb_s.dtype)
        return carry

    lax.fori_loop(0, n_chunks, state_body, 0, unroll=2)

    def readout(g, n0):
        r = pl.ds(pl.multiple_of(n0 * CHUNK, ub * CHUNK), ub * CHUNK)
        rs = pl.ds(pl.multiple_of(n0 * dk, ub * dk), ub * dk)
        cs = slice(g * dk, (g + 1) * dk)
        o = _bmm(q_s[g, r, :].reshape(ub, CHUNK, dk), sb_s[g, rs, :].reshape(ub, dk, dk))
        o = _rms(o.reshape(ub * CHUNK, dk), gain)
        o_ref[0, r, cs] = (o * _silu(z_ref[0, r, cs].astype(F32))).astype(o_ref.dtype)

    def readout_body(i, carry):
        for g in range(groups):
            readout(g, i * ub)
        return carry

    lax.fori_loop(0, n_chunks // ub, readout_body, 0)


def _gdn_core(proj, gates, conv_w, a_log, dt_bias, norm_gain, batch, seq):
    h = GDN_HEADS
    grp = GDN_GROUP
    ng = h // grp
    dk = norm_gain.shape[-1]
    n = seq // CHUNK
    p3 = proj.reshape(batch, seq, -1)
    beta_t, cum_t = _gdn_gates(gates, a_log, dt_bias, batch, seq)
    cum_r = cum_t.reshape(batch, CHUNK, h, n).transpose(0, 2, 3, 1)[:, :, :, None, :]
    head = lambda off: pl.BlockSpec((1, seq, grp * dk), lambda b, i: (b, 0, off + i))
    convw = lambda off: pl.BlockSpec((CONV_WIDTH, grp * dk), lambda b, i: (0, off + i))
    gate_spec = pl.BlockSpec((1, CHUNK, grp * n), lambda b, i: (b, 0, i))
    per_head = lambda dt: pltpu.VMEM((grp, seq, dk), dt)
    per_chunk = lambda dt: pltpu.VMEM((grp, n * dk, dk), dt)
    return pl.pallas_call(
        _gdn_kernel, grid=(batch, ng),
        in_specs=[head(0), head(ng), head(2 * ng), head(3 * ng),
                  convw(0), convw(ng), convw(2 * ng),
                  gate_spec, gate_spec,
                  pl.BlockSpec((1, grp, n, 1, CHUNK), lambda b, i: (b, i, 0, 0, 0)),
                  pl.BlockSpec((1, dk), lambda b, i: (0, 0))],
        out_specs=pl.BlockSpec((1, seq, grp * dk), lambda b, i: (b, 0, i)),
        out_shape=jax.ShapeDtypeStruct((batch, seq, h * dk), BF16),
        scratch_shapes=[per_head(BF16), per_head(F32), per_head(F32), per_chunk(BF16), per_chunk(F32),
                        per_chunk(BF16), pltpu.VMEM((grp, dk, dk), F32)],
        compiler_params=_cparams(("parallel", "parallel")), name="gdn_core",
    )(p3, p3, p3, p3, conv_w, conv_w, conv_w, beta_t, cum_t, cum_r, norm_gain.reshape(1, dk))


def _lru_kernel(x_ref, y_ref, cw_ref, cb_ref, w_ref, br_ref, bi_ref, lam_ref, o_ref, a_s, u_s):
    seq = x_ref.shape[1]
    width = x_ref.shape[2]
    x = _causal_conv(x_ref[0].astype(F32), cw_ref[...]) + cb_ref[...]
    gates = _bdot(x, w_ref[0])
    r = jax.nn.sigmoid(gates[:, :width] + br_ref[...])
    i = jax.nn.sigmoid(gates[:, width:] + bi_ref[...])
    a = jnp.exp((-LRU_C * _softplus(-lam_ref[...])) * r)
    a_s[...] = a
    t = 1.0 - a * a
    u_s[...] = jnp.where(t > 0.0, t * lax.rsqrt(t), 0.0) * (i * x)
    row = lax.broadcasted_iota(jnp.int32, (SUBLANES, width), 0)

    def scan_block(t, h_prev):
        rows = pl.ds(pl.multiple_of(t * SUBLANES, SUBLANES), SUBLANES)
        a = a_s[rows, :]
        u = u_s[rows, :]
        for d in (1, 2, 4):
            m = row >= d
            u = jnp.where(m, a * pltpu.roll(u, d, axis=0) + u, u)
            a = jnp.where(m, a * pltpu.roll(a, d, axis=0), a)
        h = u + a * h_prev
        y = jax.nn.gelu(y_ref[0, rows, :].astype(F32))
        o_ref[0, rows, :] = (h * y).astype(o_ref.dtype)
        return jnp.broadcast_to(h[SUBLANES - 1:SUBLANES, :], h.shape)

    unroll = 16

    def body(j, h_prev):
        for s in range(unroll):
            h_prev = scan_block(j * unroll + s, h_prev)
        return h_prev

    lax.fori_loop(0, seq // (SUBLANES * unroll), body, jnp.zeros((SUBLANES, width), F32))


def _lru_core(proj, conv_w, conv_b, w_r, b_r, w_i, b_i, lam, batch, seq, next_w=()):
    width = conv_w.shape[-1]
    nb = width // LRU_BLOCK
    p3 = proj.reshape(batch, seq, -1)
    w_ri = jnp.concatenate([w_r, w_i], axis=-1).astype(BF16)
    vec = lambda: pl.BlockSpec((1, LRU_BLOCK), lambda b, j: (0, j))
    ra_in, ra_out, ra_shape, ra_args = _ride_along(next_w, batch * nb, lambda b, j: b * nb + j)
    in_specs = [pl.BlockSpec((1, seq, LRU_BLOCK), lambda b, j: (b, 0, j)),
                pl.BlockSpec((1, seq, LRU_BLOCK), lambda b, j: (b, 0, nb + j)),
                pl.BlockSpec((CONV_WIDTH, LRU_BLOCK), lambda b, j: (0, j)),
                vec(),
                pl.BlockSpec((1, LRU_BLOCK, 2 * LRU_BLOCK), lambda b, j: (j, 0, 0)),
                vec(), vec(), vec()]
    res = pl.pallas_call(
        _with_ride_along(_lru_kernel, len(in_specs), 1, len(next_w)), grid=(batch, nb),
        in_specs=in_specs + ra_in,
        out_specs=[pl.BlockSpec((1, seq, LRU_BLOCK), lambda b, j: (b, 0, j))] + ra_out,
        out_shape=[jax.ShapeDtypeStruct((batch, seq, width), BF16)] + ra_shape,
        scratch_shapes=[pltpu.VMEM((seq, LRU_BLOCK), F32), pltpu.VMEM((seq, LRU_BLOCK), F32)],
        compiler_params=_cparams(("parallel", "parallel")), name="lru_core",
    )(p3, p3, conv_w, conv_b.reshape(1, width), w_ri, b_r.reshape(1, width), b_i.reshape(1, width),
      lam.reshape(1, width), *ra_args)
    return res if next_w else res[0]


def kernel(x, norm1, norm2, final_norm, ret_w_in, ret_gn_gain, ret_w_out, gdn_w_in, gdn_conv_w, gdn_a_log, gdn_dt_bias, gdn_norm_gain, gdn_w_out, gla_w_in, gla_w_gate_up, gla_gate_bias, gla_norm_gain, gla_w_out, lru_w_in, lru_conv_w, lru_conv_b, lru_w_rgate, lru_b_rgate, lru_w_igate, lru_b_igate, lru_lambda, lru_w_out, mlp_w_up, mlp_w_down):
    batch, seq, d = x.shape
    depth = norm1.shape[0]
    xf = x.reshape(batch * seq, d)
    hn = _rmsnorm(xf, norm1[0], BF16)
    w_out_all = (ret_w_out, gdn_w_out, gla_w_out, lru_w_out)
    layer_w = lambda l: ((mlp_w_up, l), (mlp_w_down, l), (w_out_all[l % 4], l // 4))
    for layer in range(depth):
        m, j = layer % 4, layer // 4
        if m == 0:
            w_in = ret_w_in[j]
            proj, wu, wd, wo = _in_proj(hn, w_in, w_in.shape[1], BF16, "ret_in", next_w=layer_w(layer))
            if layer + 1 < depth:
                o, wu_n, wd_n, wo_n = _retention_core(proj, ret_gn_gain[j], batch, seq, next_w=layer_w(layer + 1))
            else:
                o = _retention_core(proj, ret_gn_gain[j], batch, seq)
        elif m == 1:
            wu, wd, wo = wu_n, wd_n, wo_n
            w_t = gdn_w_in[j].T
            n_main = w_t.shape[0] - 2 * GDN_HEADS
            proj = _in_proj(hn, w_t, n_main, BF16, "gdn_in", w_transposed=True)
            gates = _gate_proj(hn, w_t, n_main, 2 * GDN_HEADS, "gdn_gate_in")
            o = _gdn_core(proj, gates, gdn_conv_w[j], gdn_a_log[j], gdn_dt_bias[j], gdn_norm_gain[j], batch, seq)
        elif m == 2:
            w_t = gla_w_in[j].T
            n_main = w_t.shape[0] - GLA_GATE_RANK
            proj = _in_proj(hn, w_t, n_main, BF16, "gla_in", w_transposed=True)
            gate_low = _gate_proj(hn, w_t, n_main, GLA_GATE_RANK, "gla_gate_in")
            o, wu, wd, wo = _gla_core(proj, gate_low, gla_w_gate_up[j], gla_gate_bias[j], gla_norm_gain[j],
                                      batch, seq, next_w=layer_w(layer))
        else:
            w_in = lru_w_in[j]
            proj = _in_proj(hn, w_in, w_in.shape[1], BF16, "lru_in")
            o, wu, wd, wo = _lru_core(proj, lru_conv_w[j], lru_conv_b[j], lru_w_rgate[j],
                                      lru_b_rgate[j].reshape(-1), lru_w_igate[j], lru_b_igate[j].reshape(-1),
                                      lru_lambda[j], batch, seq, next_w=layer_w(layer))
        xf, hn = _proj_res_norm(o.reshape(batch * seq, -1), wo, xf, norm2[layer])
        if layer + 1 < depth:
            xf, hn = _mlp(hn, wu, wd, xf, norm1[layer + 1], True, BF16)
        else:
            (out,) = _mlp(hn, wu, wd, xf, final_norm, False, x.dtype)
    return out.reshape(batch, seq, d)
```
